```python
import math
import jax, jax.numpy as jnp
from jax import lax
import numpy as np


D_MODEL = 4096
BATCH = 1
SEQ = 16384
DEPTH = 4

BLOCK = 128

POOL_WIDTH = D_MODEL // 4
POOL_WINDOWS = (2, 4, 8, 16)
POOL_GROUP = POOL_WIDTH // len(POOL_WINDOWS)

DIL_WIDTH = D_MODEL // 4
DIL_HEAD_DIM = 128
DIL_HEADS = DIL_WIDTH // DIL_HEAD_DIM
DIL_PATTERNS = ((128, 1), (512, 4), (2048, 16))
N_BUCKETS = 32
T5_MAX_DIST = 2048

CONV_WIDTH = D_MODEL // 4
CONV_TAPS = 3

MLA_V_DIM = 128
MLA_NOPE_DIM = 128
MLA_ROPE_DIM = 64
MLA_HEADS = (D_MODEL - POOL_WIDTH - DIL_WIDTH - CONV_WIDTH) // MLA_V_DIM
MLA_Q_RANK = 3 * D_MODEL // 16
MLA_KV_RANK = D_MODEL // 8
ROPE_THETA = 10000.0

MIX_WIDTH = POOL_WIDTH + DIL_WIDTH + CONV_WIDTH + MLA_HEADS * MLA_V_DIM
IN_SIZES = (POOL_WIDTH, DIL_WIDTH, DIL_WIDTH, DIL_WIDTH, CONV_WIDTH, CONV_WIDTH, CONV_WIDTH,
            MLA_Q_RANK, MLA_KV_RANK, MLA_ROPE_DIM)
IN_WIDTH = sum(IN_SIZES)

FFN_HIDDEN = -(-8 * D_MODEL // (3 * 256)) * 256

DEEPNORM_ALPHA = (2 * DEPTH) ** 0.25
DEEPNORM_BETA = (8 * DEPTH) ** -0.25
LN_EPS = 1e-5
RMS_EPS = 1e-6

kernel_name = "hybrid_pool_dilated_conv_mla_deepnorm"


def _layernorm(x, g, b):
    xf = x.astype(jnp.float32)
    mu = xf.mean(-1, keepdims=True)
    var = jnp.square(xf - mu).mean(-1, keepdims=True)
    return ((xf - mu) * lax.rsqrt(var + LN_EPS) * g.astype(jnp.float32) + b.astype(jnp.float32)).astype(x.dtype)


def _rmsnorm(x, g):
    xf = x.astype(jnp.float32)
    y = xf * lax.rsqrt(jnp.square(xf).mean(-1, keepdims=True) + RMS_EPS)
    return (y * g.astype(jnp.float32)).astype(x.dtype)


def _rotate(t, cos, sin):
    half = t.shape[-1] // 2
    t1, t2 = t[..., :half], t[..., half:]
    return jnp.concatenate([t1 * cos - t2 * sin, t1 * sin + t2 * cos], axis=-1)


def _t5_bucket(dist):
    exact = N_BUCKETS // 2
    d = jnp.maximum(dist, 1).astype(jnp.float32)
    large = exact + (jnp.log(d / exact) / math.log(T5_MAX_DIST / exact) * (N_BUCKETS - exact)).astype(jnp.int32)
    return jnp.where(dist < exact, dist, jnp.minimum(large, N_BUCKETS - 1))


def pool_mixer(u, w_pool, scale):
    b, s, c = u.shape
    uf = u.astype(jnp.float32)
    cs = jnp.concatenate([jnp.zeros((b, 1, c), jnp.float32), jnp.cumsum(uf, axis=1)], axis=1)
    t = jnp.arange(s)
    diffs = []
    for g, w in enumerate(POOL_WINDOWS):
        sl = slice(g * POOL_GROUP, (g + 1) * POOL_GROUP)
        csg = cs[:, :, sl]
        lower = jnp.pad(csg, ((0, 0), (w - 1, 0), (0, 0)))[:, :s]
        count = jnp.minimum(t + 1, w).astype(jnp.float32)[None, :, None]
        diffs.append((csg[:, 1:] - lower) / count - uf[:, :, sl])
    d = jnp.stack(diffs, axis=2).astype(u.dtype)
    y = jnp.einsum('bsgc,gcd->bsgd', d, w_pool).reshape(b, s, c)
    return y * scale


def _dilated_branch(q, k, v, t5_bias, window, dilation):
    b, s, h, e = q.shape
    length = s // dilation
    nb = length // BLOCK

    def to_res(t):
        return t.reshape(b, length, dilation, h, e).transpose(0, 2, 1, 3, 4).reshape(b, dilation, nb, BLOCK, h, e)

    def band(t):
        prev = jnp.concatenate([jnp.zeros_like(t[:, :, :1]), t[:, :, :-1]], axis=2)
        return jnp.concatenate([prev, t], axis=3)

    qr = to_res(q)
    kb, vb = band(to_res(k)), band(to_res(v))
    steps = window // dilation
    qi = jnp.arange(BLOCK)[:, None]
    ki = jnp.arange(2 * BLOCK)[None, :]
    dist = qi + BLOCK - ki
    key_step = jnp.arange(nb)[:, None, None] * BLOCK - BLOCK + ki[None]
    valid = (dist >= 0) & (dist <= steps) & (key_step >= 0)
    bias = t5_bias[_t5_bucket(jnp.clip(dist, 0) * dilation)]
    bias = jnp.moveaxis(bias, -1, 0)[:, None].astype(jnp.float32)
    logits = jnp.einsum('brnqhe,brnkhe->brhnqk', qr, kb).astype(jnp.float32) * (e ** -0.5) + bias
    logits = jnp.where(valid, logits, -jnp.inf)
    m = logits.max(-1)
    p = jnp.exp(logits - m[..., None])
    den = p.sum(-1)
    num = jnp.einsum('brhnqk,brnkhe->brnqhe', p.astype(v.dtype), vb).astype(jnp.float32)

    def back(t):
        return t.reshape(b, dilation, length, h, -1).transpose(0, 2, 1, 3, 4).reshape(b, s, h, -1)

    def stat(t):
        return back(jnp.moveaxis(t, 2, -1)[..., None])[..., 0]

    return back(num), stat(m), stat(den)


def dilated_attention(q, k, v, t5_bias):
    s = q.shape[1]
    span = max(d for _, d in DIL_PATTERNS) * BLOCK
    pad = (-s) % span
    padf = lambda t: jnp.pad(t, ((0, 0), (0, pad), (0, 0), (0, 0)))
    qp, kp, vp = padf(q), padf(k), padf(v)
    branches = [_dilated_branch(qp, kp, vp, t5_bias, w, d) for w, d in DIL_PATTERNS]
    m_all = jnp.stack([br[1] for br in branches])
    wts = jnp.exp(m_all - m_all.max(0))
    num = sum(wts[i][..., None] * branches[i][0] for i in range(len(branches)))
    den = sum(wts[i] * branches[i][2] for i in range(len(branches)))
    out = num / den[..., None]
    return out[:, :s].astype(q.dtype)


def short_conv(h, bg, cg, conv_w):
    z = cg * h
    s = z.shape[1]
    zp = jnp.pad(z, ((0, 0), (CONV_TAPS - 1, 0), (0, 0)))
    y = conv_w[0] * zp[:, 0:s]
    for i in range(1, CONV_TAPS):
        y = y + conv_w[i] * zp[:, i:i + s]
    return bg * y


def mla(cq, ckv, kr, q_norm_g, w_uq, kv_norm_g, w_ukv):
    b, s, _ = cq.shape
    hq = MLA_NOPE_DIM + MLA_ROPE_DIM
    q = (_rmsnorm(cq, q_norm_g) @ w_uq).reshape(b, s, MLA_HEADS, hq)
    kv = (_rmsnorm(ckv, kv_norm_g) @ w_ukv).reshape(b, s, MLA_HEADS, MLA_NOPE_DIM + MLA_V_DIM)
    q_nope, q_rope = q[..., :MLA_NOPE_DIM], q[..., MLA_NOPE_DIM:]
    k_nope, v = kv[..., :MLA_NOPE_DIM], kv[..., MLA_NOPE_DIM:]
    pos = jnp.arange(s, dtype=jnp.float32)
    inv = ROPE_THETA ** (-jnp.arange(0, MLA_ROPE_DIM, 2, dtype=jnp.float32) / MLA_ROPE_DIM)
    ang = pos[:, None] * inv[None, :]
    cos, sin = jnp.cos(ang).astype(q.dtype), jnp.sin(ang).astype(q.dtype)
    q_rope = _rotate(q_rope, cos[None, :, None], sin[None, :, None])
    k_rope = _rotate(kr, cos[None], sin[None])
    nq = s // BLOCK
    qn_blk = q_nope.reshape(b, nq, BLOCK, MLA_HEADS, MLA_NOPE_DIM).transpose(1, 0, 2, 3, 4)
    qr_blk = q_rope.reshape(b, nq, BLOCK, MLA_HEADS, MLA_ROPE_DIM).transpose(1, 0, 2, 3, 4)
    key_pos = jnp.arange(s)
    scale = hq ** -0.5

    def attend(args):
        qn, qrb, i = args
        logits = (jnp.einsum('bqhe,bkhe->bhqk', qn, k_nope)
                  + jnp.einsum('bqhe,bke->bhqk', qrb, k_rope)).astype(jnp.float32) * scale
        qpos = i * BLOCK + jnp.arange(BLOCK)
        logits = jnp.where(key_pos[None, :] <= qpos[:, None], logits, -jnp.inf)
        p = jax.nn.softmax(logits, axis=-1).astype(v.dtype)
        return jnp.einsum('bhqk,bkhe->bqhe', p, v)

    out = lax.map(attend, (qn_blk, qr_blk, jnp.arange(nq)))
    return out.transpose(1, 0, 2, 3, 4).reshape(b, s, MLA_HEADS * MLA_V_DIM)


def setup_inputs(seed: int = 0) -> dict:
    key = jax.random.key(seed)
    ks = jax.random.split(key, 18)
    f32 = jnp.float32

    def nrm(k, shape, scale):
        return jax.random.normal(k, shape, f32) * scale

    L = DEPTH
    return {
        "x": nrm(ks[0], (BATCH, SEQ, D_MODEL), 1.0),
        "w_in": nrm(ks[1], (L, D_MODEL, IN_WIDTH), D_MODEL ** -0.5),
        "w_pool": nrm(ks[2], (L, len(POOL_WINDOWS), POOL_GROUP, POOL_GROUP), POOL_GROUP ** -0.5),
        "pool_scale": 1.0 + nrm(ks[3], (L, POOL_WIDTH), 0.02),
        "t5_bias": nrm(ks[4], (N_BUCKETS, DIL_HEADS), 0.2),
        "conv_w": nrm(ks[5], (L, CONV_TAPS, CONV_WIDTH), CONV_TAPS ** -0.5),
        "q_norm_g": 1.0 + nrm(ks[6], (L, MLA_Q_RANK), 0.02),
        "w_uq": nrm(ks[7], (L, MLA_Q_RANK, MLA_HEADS * (MLA_NOPE_DIM + MLA_ROPE_DIM)), MLA_Q_RANK ** -0.5),
        "kv_norm_g": 1.0 + nrm(ks[8], (L, MLA_KV_RANK), 0.02),
        "w_ukv": nrm(ks[9], (L, MLA_KV_RANK, MLA_HEADS * (MLA_NOPE_DIM + MLA_V_DIM)), MLA_KV_RANK ** -0.5),
        "w_out": nrm(ks[10], (L, MIX_WIDTH, D_MODEL), DEEPNORM_BETA * MIX_WIDTH ** -0.5),
        "ln1_g": 1.0 + nrm(ks[11], (L, D_MODEL), 0.02),
        "ln1_b": nrm(ks[12], (L, D_MODEL), 0.02),
        "w_gate": nrm(ks[13], (L, D_MODEL, FFN_HIDDEN), D_MODEL ** -0.5),
        "w_up": nrm(ks[14], (L, D_MODEL, FFN_HIDDEN), D_MODEL ** -0.5),
        "w_down": nrm(ks[15], (L, FFN_HIDDEN, D_MODEL), DEEPNORM_BETA * FFN_HIDDEN ** -0.5),
        "ln2_g": 1.0 + nrm(ks[16], (L, D_MODEL), 0.02),
        "ln2_b": nrm(ks[17], (L, D_MODEL), 0.02),
    }


def reference(x, w_in, w_pool, pool_scale, t5_bias, conv_w, q_norm_g, w_uq, kv_norm_g, w_ukv,
              w_out, ln1_g, ln1_b, w_gate, w_up, w_down, ln2_g, ln2_b):
    b, s, _ = x.shape
    split_points = [int(i) for i in np.cumsum(IN_SIZES)[:-1]]
    for l in range(DEPTH):
        proj = x @ w_in[l]
        u, q, k, v, hc, bg, cg, cq, ckv, kr = jnp.split(proj, split_points, axis=-1)
        y_a = pool_mixer(u, w_pool[l], pool_scale[l])
        heads = lambda t: t.reshape(b, s, DIL_HEADS, DIL_HEAD_DIM)
        y_b = dilated_attention(heads(q), heads(k), heads(v), t5_bias).reshape(b, s, DIL_WIDTH)
        y_c = short_conv(hc, bg, cg, conv_w[l])
        y_d = mla(cq, ckv, kr, q_norm_g[l], w_uq[l], kv_norm_g[l], w_ukv[l])
        mixed = jnp.concatenate([y_a, y_b, y_c, y_d], axis=-1) @ w_out[l]
        x = _layernorm(DEEPNORM_ALPHA * x + mixed, ln1_g[l], ln1_b[l])
        hid = jax.nn.silu(x @ w_gate[l]) * (x @ w_up[l])
        x = _layernorm(DEEPNORM_ALPHA * x + hid @ w_down[l], ln2_g[l], ln2_b[l])
    return x
```

```python
import functools
import math

import jax
import jax.numpy as jnp
from jax import lax
from jax.experimental import pallas as pl
from jax.experimental.pallas import tpu as pltpu

BLOCK = 128
HEAD_DIM = 128
ROPE_DIM = 64
MLA_QK_PAD = 256
POOL_WINDOWS = (2, 4, 8, 16)
DIL_PATTERNS = ((128, 1), (512, 4), (2048, 16))
N_BUCKETS = 32
T5_MAX_DIST = 2048
CONV_TAPS = 3
ROPE_THETA = 10000.0
LN_EPS = 1e-5
RMS_EPS = 1e-6
HALO = 16
NEG = -1e30

V7X_VMEM_BUDGET = 60 * 1024 * 1024
VMEM_SLACK = 8 * 1024 * 1024

F32 = jnp.float32
BF16 = jnp.bfloat16


def _params(sem, vmem_bytes):
    return pltpu.CompilerParams(dimension_semantics=sem,
                                vmem_limit_bytes=int(min(vmem_bytes + VMEM_SLACK, V7X_VMEM_BUDGET)))


def _nbytes(shape, dtype):
    return math.prod(shape) * jnp.dtype(dtype).itemsize


def _pick(n, prefs):
    for p in prefs:
        if n % p == 0:
            return p
    return n


def _mm_kernel(a_ref, w_ref, o_ref, *, nk):
    part = jnp.dot(a_ref[...], w_ref[...], preferred_element_type=F32)
    if nk == 1:
        o_ref[...] = part.astype(o_ref.dtype)
    else:
        k = pl.program_id(2)

        @pl.when(k == 0)
        def _():
            o_ref[...] = part

        @pl.when(k > 0)
        def _():
            o_ref[...] += part


def _matmul(a, w, out_dtype, bm, bn, bk=None, name="matmul"):
    m, kdim = a.shape
    n = w.shape[1]
    bk = kdim if bk is None else bk
    nk = kdim // bk
    assert m % bm == 0 and n % bn == 0 and kdim % bk == 0
    assert nk == 1 or out_dtype == F32
    vmem = 2 * (_nbytes((bm, bk), a.dtype) + _nbytes((bk, bn), w.dtype) + _nbytes((bm, bn), out_dtype)) \
        + 2 * _nbytes((bm, bn), F32)
    return pl.pallas_call(
        functools.partial(_mm_kernel, nk=nk),
        out_shape=jax.ShapeDtypeStruct((m, n), out_dtype),
        grid=(m // bm, n // bn, nk),
        in_specs=[pl.BlockSpec((bm, bk), lambda i, j, k: (i, k)),
                  pl.BlockSpec((bk, bn), lambda i, j, k: (k, j))],
        out_specs=pl.BlockSpec((bm, bn), lambda i, j, k: (i, j)),
        compiler_params=_params(("parallel", "parallel", "arbitrary"), vmem),
        name=name,
    )(a, w)


def _gate_up_kernel(a_ref, wg_ref, wu_ref, o_ref):
    a = a_ref[...]
    g = jnp.dot(a, wg_ref[...], preferred_element_type=F32)
    u = jnp.dot(a, wu_ref[...], preferred_element_type=F32)
    o_ref[...] = (g * (1.0 / (1.0 + jnp.exp(-g))) * u).astype(o_ref.dtype)


def _gate_up(a, wg, wu, bm, bn):
    m, kdim = a.shape
    n = wg.shape[1]
    vmem = 2 * (_nbytes((bm, kdim), a.dtype) + 2 * _nbytes((kdim, bn), wg.dtype) + _nbytes((bm, bn), BF16)) \
        + 4 * _nbytes((bm, bn), F32)
    return pl.pallas_call(
        _gate_up_kernel,
        out_shape=jax.ShapeDtypeStruct((m, n), BF16),
        grid=(m // bm, n // bn),
        in_specs=[pl.BlockSpec((bm, kdim), lambda i, j: (i, 0)),
                  pl.BlockSpec((kdim, bn), lambda i, j: (0, j)),
                  pl.BlockSpec((kdim, bn), lambda i, j: (0, j))],
        out_specs=pl.BlockSpec((bm, bn), lambda i, j: (i, j)),
        compiler_params=_params(("parallel", "parallel"), vmem),
        name="ffn_gate_up",
    )(a, wg, wu)


def _add_ln_kernel(y_ref, x_ref, g_ref, b_ref, of_ref, ob_ref, *, alpha):
    z = alpha * x_ref[...] + y_ref[...]
    mu = jnp.mean(z, axis=-1, keepdims=True)
    zc = z - mu
    var = jnp.mean(zc * zc, axis=-1, keepdims=True)
    out = zc * lax.rsqrt(var + LN_EPS) * g_ref[...] + b_ref[...]
    of_ref[...] = out
    ob_ref[...] = out.astype(BF16)


def _add_ln(y, x, g, b, alpha, br):
    s, d = x.shape
    row = pl.BlockSpec((br, d), lambda i: (i, 0))
    vec = pl.BlockSpec((1, d), lambda i: (0, 0))
    vmem = 2 * (3 * _nbytes((br, d), F32) + _nbytes((br, d), BF16)) + 4 * _nbytes((br, d), F32)
    return pl.pallas_call(
        functools.partial(_add_ln_kernel, alpha=alpha),
        out_shape=(jax.ShapeDtypeStruct((s, d), F32), jax.ShapeDtypeStruct((s, d), BF16)),
        grid=(s // br,),
        in_specs=[row, row, vec, vec],
        out_specs=(row, row),
        compiler_params=_params(("parallel",), vmem),
        name="add_layernorm",
    )(y, x, g.reshape(1, d), b.reshape(1, d))


def _local_kernel(u_ref, uh_ref, hc_ref, hch_ref, bg_ref, cg_ref, cgh_ref, wp_ref, ps_ref, cw_ref,
                  ya_ref, yc_ref, us_ref, zs_ref, *, bs, group):
    i = pl.program_id(0)
    has_hist = i > 0
    us_ref[0:HALO, :] = jnp.where(has_hist, uh_ref[...].astype(F32), 0.0)
    us_ref[HALO:, :] = u_ref[...].astype(F32)
    zs_ref[0:HALO, :] = jnp.where(has_hist, cgh_ref[...].astype(F32) * hch_ref[...].astype(F32), 0.0)
    zs_ref[HALO:, :] = cg_ref[...].astype(F32) * hc_ref[...].astype(F32)

    pos = i * bs + lax.broadcasted_iota(jnp.int32, (bs, group), 0)
    for g, w in enumerate(POOL_WINDOWS):
        cols = slice(g * group, (g + 1) * group)
        cur = us_ref[HALO:HALO + bs, cols]
        acc = cur
        for j in range(1, w):
            acc = acc + us_ref[HALO - j:HALO - j + bs, cols]
        count = jnp.minimum(pos + 1, w).astype(F32)
        diff = acc / count - cur
        y = jnp.dot(diff.astype(BF16), wp_ref[g], preferred_element_type=F32)
        ya_ref[:, cols] = (y * ps_ref[:, cols]).astype(ya_ref.dtype)

    conv = cw_ref[0:1, :] * zs_ref[HALO - 2:HALO - 2 + bs, :]
    for tap in range(1, CONV_TAPS):
        off = HALO - (CONV_TAPS - 1) + tap
        conv = conv + cw_ref[tap:tap + 1, :] * zs_ref[off:off + bs, :]
    yc_ref[...] = (bg_ref[...].astype(F32) * conv).astype(yc_ref.dtype)


def _local_mixers(proj, w_pool, pool_scale, conv_w, width, bs):
    s = proj.shape[0]
    group = width // len(POOL_WINDOWS)
    hb = bs // HALO

    def blk(c):
        return pl.BlockSpec((bs, width), lambda i, c=c: (i, c))

    def halo(c):
        return pl.BlockSpec((HALO, width), lambda i, c=c: (jnp.maximum(i * hb - 1, 0), c))

    out = pl.BlockSpec((bs, width), lambda i: (i, 0))
    vmem = 2 * (6 * _nbytes((bs, width), BF16) + _nbytes(w_pool.shape, BF16)) \
        + 2 * _nbytes((bs + HALO, width), F32) + 6 * _nbytes((bs, width), F32)
    return pl.pallas_call(
        functools.partial(_local_kernel, bs=bs, group=group),
        out_shape=(jax.ShapeDtypeStruct((s, width), BF16), jax.ShapeDtypeStruct((s, width), BF16)),
        grid=(s // bs,),
        in_specs=[blk(0), halo(0), blk(4), halo(4), blk(5), blk(6), halo(6),
                  pl.BlockSpec(w_pool.shape, lambda i: (0, 0, 0)),
                  pl.BlockSpec((1, width), lambda i: (0, 0)),
                  pl.BlockSpec((CONV_TAPS, width), lambda i: (0, 0))],
        out_specs=(out, out),
        scratch_shapes=[pltpu.VMEM((bs + HALO, width), F32), pltpu.VMEM((bs + HALO, width), F32)],
        compiler_params=_params(("arbitrary",), vmem),
        name="pool_conv_mixers",
    )(proj, proj, proj, proj, proj, proj, proj, w_pool, pool_scale.reshape(1, width), conv_w)


def _t5_bucket(dist):
    exact = N_BUCKETS // 2
    d = jnp.maximum(dist, 1).astype(F32)
    large = exact + (jnp.log(d / exact) / math.log(T5_MAX_DIST / exact) * (N_BUCKETS - exact)).astype(jnp.int32)
    return jnp.where(dist < exact, dist, jnp.minimum(large, N_BUCKETS - 1))


def _band_bias(t5_bias, window, dilation):
    steps = window // dilation
    qi = jnp.arange(BLOCK)[:, None]
    ki = jnp.arange(2 * BLOCK)[None, :]
    dist = qi + BLOCK - ki
    bias = t5_bias[_t5_bucket(jnp.clip(dist, 0) * dilation)].astype(F32)
    bias = jnp.where(((dist >= 0) & (dist <= steps))[..., None], bias, NEG)
    return jnp.moveaxis(bias, -1, 0)


def _dilated_kernel(*refs, rb, heads, has_prev, last):
    q_ref, kc_ref, kh_ref, vc_ref, vh_ref, bias_ref = refs[:6]
    refs = refs[6:]
    if has_prev:
        op_ref, lp_ref = refs[:2]
        refs = refs[2:]
    if last:
        o_ref, k_scr, v_scr = refs
    else:
        o_ref, l_ref, k_scr, v_scr = refs

    n = pl.program_id(1)
    k_scr[0:BLOCK, :] = kh_ref[...]
    k_scr[BLOCK:, :] = kc_ref[...]
    v_scr[0:BLOCK, :] = vh_ref[...]
    v_scr[BLOCK:, :] = vc_ref[...]
    scale = HEAD_DIM ** -0.5
    before_start = (n == 0) & (lax.broadcasted_iota(jnp.int32, (BLOCK, 2 * BLOCK), 1) < BLOCK)

    for b in range(rb // BLOCK):
        rows = slice(b * BLOCK, (b + 1) * BLOCK)
        band = slice(b * BLOCK, (b + 2) * BLOCK)
        for h in range(heads):
            cols = slice(h * HEAD_DIM, (h + 1) * HEAD_DIM)
            s = lax.dot_general(q_ref[rows, cols], k_scr[band, cols], (((1,), (1,)), ((), ())),
                                preferred_element_type=F32)
            s = s * scale + bias_ref[h]
            if b == 0:
                s = jnp.where(before_start, NEG, s)
            m = jnp.max(s, axis=-1, keepdims=True)
            p = jnp.exp(s - m)
            den = jnp.sum(p, axis=-1, keepdims=True)
            out = jnp.dot(p.astype(BF16), v_scr[band, cols], preferred_element_type=F32) / den
            lse = m + jnp.log(den)
            if has_prev:
                lse_prev = lp_ref[rows, cols]
                top = jnp.maximum(lse_prev, lse)
                w_prev = jnp.exp(lse_prev - top)
                w_cur = jnp.exp(lse - top)
                tot = w_prev + w_cur
                out = (w_prev * op_ref[rows, cols] + w_cur * out) / tot
                lse = top + jnp.log(tot)
            o_ref[rows, cols] = out.astype(o_ref.dtype)
            if not last:
                l_ref[rows, cols] = jnp.broadcast_to(lse, (BLOCK, HEAD_DIM))


def _dilated_branch(proj, bias, dilation, width, prev, last):
    s = proj.shape[0]
    d = dilation
    length = s // d
    heads = width // HEAD_DIM
    rb = _pick(length, (512, 256, 128))
    nsub = rb // BLOCK
    view = proj.reshape(length, d * 7 * width)

    def cur(c):
        return pl.BlockSpec((rb, width), lambda r, n, c=c: (n, r * 7 + c))

    def hist(c):
        return pl.BlockSpec((BLOCK, width), lambda r, n, c=c: (jnp.maximum(n * nsub - 1, 0), r * 7 + c))

    res = pl.BlockSpec((rb, width), lambda r, n: (n, r))
    in_specs = [cur(1), cur(2), hist(2), cur(3), hist(3), pl.BlockSpec(bias.shape, lambda r, n: (0, 0, 0))]
    args = [view, view, view, view, view, bias]
    if prev is not None:
        in_specs += [res, res]
        args += [prev[0].reshape(length, d * width), prev[1].reshape(length, d * width)]
    if last:
        out_shape = jax.ShapeDtypeStruct((length, d * width), BF16)
        out_specs = res
    else:
        out_shape = (jax.ShapeDtypeStruct((length, d * width), F32),) * 2
        out_specs = (res, res)
    vmem = 2 * (5 * _nbytes((rb, width), BF16) + _nbytes(bias.shape, F32) + 4 * _nbytes((rb, width), F32)) \
        + 2 * _nbytes((rb + BLOCK, width), BF16) + 16 * _nbytes((BLOCK, 2 * BLOCK), F32) * heads
    result = pl.pallas_call(
        functools.partial(_dilated_kernel, rb=rb, heads=heads, has_prev=prev is not None, last=last),
        out_shape=out_shape,
        grid=(d, length // rb),
        in_specs=in_specs,
        out_specs=out_specs,
        scratch_shapes=[pltpu.VMEM((rb + BLOCK, width), BF16), pltpu.VMEM((rb + BLOCK, width), BF16)],
        compiler_params=_params(("parallel", "arbitrary"), vmem),
        name=f"dilated_attention_d{d}",
    )(*args)
    if last:
        return result.reshape(s, width)
    return result[0].reshape(s, width), result[1].reshape(s, width)


def _dilated_attention(proj, t5_bias, width):
    state = None
    order = sorted(DIL_PATTERNS, key=lambda p: -p[1])
    for idx, (window, dilation) in enumerate(order):
        assert window // dilation == BLOCK and proj.shape[0] % (dilation * BLOCK) == 0
        state = _dilated_branch(proj, _band_bias(t5_bias, window, dilation), dilation, width, state,
                                last=idx == len(order) - 1)
    return state


def _rms(x, g):
    return x * lax.rsqrt(jnp.mean(x * x, axis=-1, keepdims=True) + RMS_EPS) * g


def _rope_chunk(chunk, cos_ref, sin_ref):
    half = ROPE_DIM // 2
    swapped = pltpu.roll(chunk, HEAD_DIM - half, 1) + pltpu.roll(chunk, half, 1)
    return chunk * cos_ref[...] + swapped * sin_ref[...]


def _mla_q_kernel(cq_ref, g_ref, w_ref, cos_ref, sin_ref, q_ref, *, heads, scale):
    qn = _rms(cq_ref[...], g_ref[...]).astype(BF16)
    q = jnp.dot(qn, w_ref[...], preferred_element_type=F32)
    for h in range(heads):
        nope = slice(h * MLA_QK_PAD, h * MLA_QK_PAD + HEAD_DIM)
        rope = slice(h * MLA_QK_PAD + HEAD_DIM, (h + 1) * MLA_QK_PAD)
        q_ref[:, nope] = (q[:, nope] * scale).astype(q_ref.dtype)
        q_ref[:, rope] = (_rope_chunk(q[:, rope], cos_ref, sin_ref) * scale).astype(q_ref.dtype)


def _mla_kv_kernel(ckv_ref, kr_ref, g_ref, w_ref, cos_ref, sin_ref, k_ref, v_ref, *, heads):
    kvn = _rms(ckv_ref[...], g_ref[...]).astype(BF16)
    kv = jnp.dot(kvn, w_ref[...], preferred_element_type=F32)
    k_rope = _rope_chunk(kr_ref[...], cos_ref, sin_ref).astype(k_ref.dtype)
    for h in range(heads):
        k_ref[:, h * MLA_QK_PAD:h * MLA_QK_PAD + HEAD_DIM] = \
            kv[:, h * HEAD_DIM:(h + 1) * HEAD_DIM].astype(k_ref.dtype)
        k_ref[:, h * MLA_QK_PAD + HEAD_DIM:(h + 1) * MLA_QK_PAD] = k_rope
    v_ref[...] = kv[:, heads * HEAD_DIM:].astype(v_ref.dtype)


def _mla_flash_kernel(q_ref, k_ref, v_ref, o_ref, m_scr, l_scr, acc_scr, *, bq):
    qi = pl.program_id(1)
    m_scr[...] = jnp.full(m_scr.shape, NEG, F32)
    l_scr[...] = jnp.zeros(l_scr.shape, F32)
    acc_scr[...] = jnp.zeros(acc_scr.shape, F32)
    q = q_ref[...]

    def step(j, diagonal):
        start = pl.multiple_of(j * bq, bq)
        s = lax.dot_general(q, k_ref[pl.ds(start, bq), :], (((1,), (1,)), ((), ())),
                            preferred_element_type=F32)
        if diagonal:
            row = lax.broadcasted_iota(jnp.int32, (bq, bq), 0)
            col = lax.broadcasted_iota(jnp.int32, (bq, bq), 1)
            s = jnp.where(col <= row, s, NEG)
        m_prev = m_scr[...]
        m_new = jnp.maximum(m_prev, jnp.max(s, axis=-1, keepdims=True))
        alpha = jnp.exp(m_prev - m_new)
        p = jnp.exp(s - m_new[:, 0:1])
        l_scr[...] = alpha * l_scr[...] + jnp.sum(p, axis=-1, keepdims=True)
        acc_scr[...] = alpha * acc_scr[...] + jnp.dot(p.astype(BF16), v_ref[pl.ds(start, bq), :],
                                                     preferred_element_type=F32)
        m_scr[...] = m_new

    def body(j, carry):
        step(j, False)
        return carry

    lax.fori_loop(0, qi, body, 0)
    step(qi, True)
    o_ref[...] = (acc_scr[...] / l_scr[...]).astype(o_ref.dtype)


def _rope_tables(s):
    half = ROPE_DIM // 2
    pos = jnp.arange(s, dtype=F32)
    inv = ROPE_THETA ** (-jnp.arange(0, ROPE_DIM, 2, dtype=F32) / ROPE_DIM)
    ang = pos[:, None] * inv[None, :]
    cos, sin = jnp.cos(ang), jnp.sin(ang)
    zeros = jnp.zeros((s, HEAD_DIM - 2 * half), F32)
    return jnp.concatenate([cos, cos, zeros], axis=1), jnp.concatenate([-sin, sin, zeros], axis=1)


def _mla(lat, q_rank, kv_rank, q_norm_g, w_uq_p, kv_norm_g, w_ukv_p, cos_t, sin_t, heads):
    s = lat.shape[0]
    bm = _pick(s, (1024, 512, 256, 128))
    assert (q_rank - kv_rank) % HEAD_DIM == 0 and kv_rank % HEAD_DIM == 0
    tab = pl.BlockSpec((bm, HEAD_DIM), lambda i: (i, 0))
    qk_width = heads * MLA_QK_PAD
    v_width = heads * HEAD_DIM
    scale = (HEAD_DIM + ROPE_DIM) ** -0.5

    vmem_q = 2 * (_nbytes((bm, q_rank), F32) + _nbytes(w_uq_p.shape, BF16) + 2 * _nbytes((bm, HEAD_DIM), F32)
                  + _nbytes((bm, qk_width), BF16)) + 3 * _nbytes((bm, qk_width), F32)
    q = pl.pallas_call(
        functools.partial(_mla_q_kernel, heads=heads, scale=scale),
        out_shape=jax.ShapeDtypeStruct((s, qk_width), BF16),
        grid=(s // bm,),
        in_specs=[pl.BlockSpec((bm, q_rank), lambda i: (i, 1)),
                  pl.BlockSpec((1, q_rank), lambda i: (0, 0)),
                  pl.BlockSpec(w_uq_p.shape, lambda i: (0, 0)), tab, tab],
        out_specs=pl.BlockSpec((bm, qk_width), lambda i: (i, 0)),
        compiler_params=_params(("parallel",), vmem_q),
        name="mla_q_up",
    )(lat, q_norm_g.reshape(1, q_rank), w_uq_p, cos_t, sin_t)

    vmem_kv = 2 * (_nbytes((bm, kv_rank), F32) + _nbytes(w_ukv_p.shape, BF16) + 3 * _nbytes((bm, HEAD_DIM), F32)
                   + _nbytes((bm, qk_width), BF16) + _nbytes((bm, v_width), BF16)) \
        + 3 * _nbytes((bm, 2 * v_width), F32)
    k, v = pl.pallas_call(
        functools.partial(_mla_kv_kernel, heads=heads),
        out_shape=(jax.ShapeDtypeStruct((s, qk_width), BF16), jax.ShapeDtypeStruct((s, v_width), BF16)),
        grid=(s // bm,),
        in_specs=[pl.BlockSpec((bm, kv_rank), lambda i: (i, 0)),
                  pl.BlockSpec((bm, HEAD_DIM), lambda i: (i, kv_rank // HEAD_DIM)),
                  pl.BlockSpec((1, kv_rank), lambda i: (0, 0)),
                  pl.BlockSpec(w_ukv_p.shape, lambda i: (0, 0)), tab, tab],
        out_specs=(pl.BlockSpec((bm, qk_width), lambda i: (i, 0)),
                   pl.BlockSpec((bm, v_width), lambda i: (i, 0))),
        compiler_params=_params(("parallel",), vmem_kv),
        name="mla_kv_up",
    )(lat, lat, kv_norm_g.reshape(1, kv_rank), w_ukv_p, cos_t, sin_t)

    bq = _pick(s, (512, 256, 128))
    vmem_a = 2 * (_nbytes((bq, MLA_QK_PAD), BF16) + _nbytes((s, MLA_QK_PAD), BF16) + _nbytes((s, HEAD_DIM), BF16)
                  + _nbytes((bq, HEAD_DIM), BF16)) + 3 * _nbytes((bq, HEAD_DIM), F32) + 6 * _nbytes((bq, bq), F32)
    return pl.pallas_call(
        functools.partial(_mla_flash_kernel, bq=bq),
        out_shape=jax.ShapeDtypeStruct((s, v_width), BF16),
        grid=(heads, s // bq),
        in_specs=[pl.BlockSpec((bq, MLA_QK_PAD), lambda h, i: (i, h)),
                  pl.BlockSpec((s, MLA_QK_PAD), lambda h, i: (0, h)),
                  pl.BlockSpec((s, HEAD_DIM), lambda h, i: (0, h))],
        out_specs=pl.BlockSpec((bq, HEAD_DIM), lambda h, i: (i, h)),
        scratch_shapes=[pltpu.VMEM((bq, HEAD_DIM), F32)] * 3,
        compiler_params=_params(("parallel", "arbitrary"), vmem_a),
        name="mla_flash_attention",
    )(q, k, v)


def _pad_cols(w, width):
    return jnp.pad(w, ((0, 0), (0, width - w.shape[1])))


def _prep_layer(l, d, w_in, w_uq, w_ukv, w_gate, w_up, w_down, heads, f_pad):
    width = d // 4
    q_rank = w_uq.shape[1]
    kv_rank = w_ukv.shape[1]
    n_main = 7 * width
    w = w_in[l]
    w_main = w[:, :n_main].astype(BF16)
    cq = w[:, n_main:n_main + q_rank]
    ckv = w[:, n_main + q_rank:n_main + q_rank + kv_rank]
    kr = w[:, n_main + q_rank + kv_rank:]
    w_lat = jnp.concatenate([ckv, _pad_cols(kr, q_rank - kv_rank), cq], axis=1).astype(BF16)
    uq = w_uq[l].reshape(q_rank, heads, HEAD_DIM + ROPE_DIM)
    uq = jnp.pad(uq, ((0, 0), (0, 0), (0, MLA_QK_PAD - HEAD_DIM - ROPE_DIM))).reshape(q_rank, heads * MLA_QK_PAD)
    ukv = w_ukv[l].reshape(kv_rank, heads, 2, HEAD_DIM).transpose(0, 2, 1, 3).reshape(kv_rank, 2 * heads * HEAD_DIM)
    f = w_gate.shape[-1]
    return dict(
        w_main=w_main, w_lat=w_lat, w_uq=uq.astype(BF16), w_ukv=ukv.astype(BF16),
        w_gate=_pad_cols(w_gate[l], f_pad).astype(BF16), w_up=_pad_cols(w_up[l], f_pad).astype(BF16),
        w_down=jnp.pad(w_down[l], ((0, f_pad - f), (0, 0))).astype(BF16),
    )


def kernel(x, w_in, w_pool, pool_scale, t5_bias, conv_w, q_norm_g, w_uq, kv_norm_g, w_ukv, w_out, ln1_g, ln1_b,
           w_gate, w_up, w_down, ln2_g, ln2_b):
    batch, s, d = x.shape
    depth = w_in.shape[0]
    width = d // 4
    heads = (d - 3 * width) // HEAD_DIM
    q_rank = w_uq.shape[1]
    kv_rank = w_ukv.shape[1]
    f = w_gate.shape[-1]
    f_pad = -(-f // 1024) * 1024
    alpha = (2 * depth) ** 0.25
    assert batch == 1 and width % (len(POOL_WINDOWS) * HEAD_DIM) == 0

    bm = _pick(s, (1024, 512, 256, 128))
    cos_t, sin_t = _rope_tables(s)
    xf = x.reshape(s, d)
    xb = xf.astype(BF16)
    for l in range(depth):
        p = _prep_layer(l, d, w_in, w_uq, w_ukv, w_gate, w_up, w_down, heads, f_pad)
        proj = _matmul(xb, p["w_main"], BF16, bm, _pick(7 * width, (1024, 896, 512, 256, 128)), name="in_proj_main")
        lat = _matmul(xb, p["w_lat"], F32, bm, _pick(2 * q_rank, (768, 512, 384, 256, 128)), name="in_proj_latent")
        y_a, y_c = _local_mixers(proj, w_pool[l].astype(BF16), pool_scale[l], conv_w[l], width,
                                 _pick(s, (512, 256, 128)))
        y_b = _dilated_attention(proj, t5_bias, width)
        y_d = _mla(lat, q_rank, kv_rank, q_norm_g[l], p["w_uq"], kv_norm_g[l], p["w_ukv"], cos_t, sin_t, heads)
        mixed = jnp.concatenate([y_a, y_b, y_c, y_d], axis=-1)
        y = _matmul(mixed, w_out[l].astype(BF16), F32, bm, _pick(d, (1024, 512, 256, 128)), name="out_proj")
        xf, xb = _add_ln(y, xf, ln1_g[l], ln1_b[l], alpha, _pick(s, (256, 128)))
        hid = _gate_up(xb, p["w_gate"], p["w_up"], bm, _pick(f_pad, (512, 256, 128)))
        y = _matmul(hid, p["w_down"], F32, bm, _pick(d, (1024, 512, 256, 128)),
                    bk=_pick(f_pad, (2816, 2048, 1024, 512)), name="ffn_down")
        xf, xb = _add_ln(y, xf, ln2_g[l], ln2_b[l], alpha, _pick(s, (256, 128)))
    return xf.reshape(batch, s, d)
```

```python
import functools
import math

import jax
import jax.numpy as jnp
from jax import lax
from jax.experimental import pallas as pl
from jax.experimental.pallas import tpu as pltpu

BLOCK = 128
HEAD_DIM = 128
ROPE_DIM = 64
MLA_QK_PAD = 256
POOL_WINDOWS = (2, 4, 8, 16)
DIL_PATTERNS = ((128, 1), (512, 4), (2048, 16))
N_BUCKETS = 32
T5_MAX_DIST = 2048
CONV_TAPS = 3
ROPE_THETA = 10000.0
LN_EPS = 1e-5
RMS_EPS = 1e-6
HALO = 16
NEG = -1e30

V7X_VMEM_BUDGET = 60 * 1024 * 1024
VMEM_SLACK = 8 * 1024 * 1024

F32 = jnp.float32
BF16 = jnp.bfloat16


def _params(sem, vmem_bytes):
    return pltpu.CompilerParams(dimension_semantics=sem,
                                vmem_limit_bytes=int(min(vmem_bytes + VMEM_SLACK, V7X_VMEM_BUDGET)))


def _nbytes(shape, dtype):
    return math.prod(shape) * jnp.dtype(dtype).itemsize


def _pick(n, prefs):
    for p in prefs:
        if n % p == 0:
            return p
    return n


def _mm_kernel(a_ref, w_ref, o_ref, *, nk):
    part = jnp.dot(a_ref[...], w_ref[...], preferred_element_type=F32)
    if nk == 1:
        o_ref[...] = part.astype(o_ref.dtype)
    else:
        k = pl.program_id(2)

        @pl.when(k == 0)
        def _():
            o_ref[...] = part

        @pl.when(k > 0)
        def _():
            o_ref[...] += part


def _matmul(a, w, out_dtype, bm, bn, bk=None, name="matmul"):
    m, kdim = a.shape
    n = w.shape[1]
    bk = kdim if bk is None else bk
    nk = kdim // bk
    assert m % bm == 0 and n % bn == 0 and kdim % bk == 0
    assert nk == 1 or out_dtype == F32
    vmem = 2 * (_nbytes((bm, bk), a.dtype) + _nbytes((bk, bn), w.dtype) + _nbytes((bm, bn), out_dtype)) \
        + 2 * _nbytes((bm, bn), F32)
    return pl.pallas_call(
        functools.partial(_mm_kernel, nk=nk),
        out_shape=jax.ShapeDtypeStruct((m, n), out_dtype),
        grid=(m // bm, n // bn, nk),
        in_specs=[pl.BlockSpec((bm, bk), lambda i, j, k: (i, k)),
                  pl.BlockSpec((bk, bn), lambda i, j, k: (k, j))],
        out_specs=pl.BlockSpec((bm, bn), lambda i, j, k: (i, j)),
        compiler_params=_params(("parallel", "parallel", "arbitrary"), vmem),
        name=name,
    )(a, w)


def _gate_up_kernel(a_ref, wg_ref, wu_ref, o_ref):
    a = a_ref[...]
    g = jnp.dot(a, wg_ref[...], preferred_element_type=F32)
    u = jnp.dot(a, wu_ref[...], preferred_element_type=F32)
    o_ref[...] = (g * (1.0 / (1.0 + jnp.exp(-g))) * u).astype(o_ref.dtype)


def _gate_up(a, wg, wu, bm, bn):
    m, kdim = a.shape
    n = wg.shape[1]
    vmem = 2 * (_nbytes((bm, kdim), a.dtype) + 2 * _nbytes((kdim, bn), wg.dtype) + _nbytes((bm, bn), BF16)) \
        + 4 * _nbytes((bm, bn), F32)
    return pl.pallas_call(
        _gate_up_kernel,
        out_shape=jax.ShapeDtypeStruct((m, n), BF16),
        grid=(m // bm, n // bn),
        in_specs=[pl.BlockSpec((bm, kdim), lambda i, j: (i, 0)),
                  pl.BlockSpec((kdim, bn), lambda i, j: (0, j)),
                  pl.BlockSpec((kdim, bn), lambda i, j: (0, j))],
        out_specs=pl.BlockSpec((bm, bn), lambda i, j: (i, j)),
        compiler_params=_params(("parallel", "parallel"), vmem),
        name="ffn_gate_up",
    )(a, wg, wu)


def _add_ln_kernel(y_ref, x_ref, g_ref, b_ref, of_ref, ob_ref, *, alpha):
    z = alpha * x_ref[...] + y_ref[...]
    mu = jnp.mean(z, axis=-1, keepdims=True)
    zc = z - mu
    var = jnp.mean(zc * zc, axis=-1, keepdims=True)
    out = zc * lax.rsqrt(var + LN_EPS) * g_ref[...] + b_ref[...]
    of_ref[...] = out
    ob_ref[...] = out.astype(BF16)


def _add_ln(y, x, g, b, alpha, br):
    s, d = x.shape
    row = pl.BlockSpec((br, d), lambda i: (i, 0))
    vec = pl.BlockSpec((1, d), lambda i: (0, 0))
    vmem = 2 * (3 * _nbytes((br, d), F32) + _nbytes((br, d), BF16)) + 4 * _nbytes((br, d), F32)
    return pl.pallas_call(
        functools.partial(_add_ln_kernel, alpha=alpha),
        out_shape=(jax.ShapeDtypeStruct((s, d), F32), jax.ShapeDtypeStruct((s, d), BF16)),
        grid=(s // br,),
        in_specs=[row, row, vec, vec],
        out_specs=(row, row),
        compiler_params=_params(("parallel",), vmem),
        name="add_layernorm",
    )(y, x, g.reshape(1, d), b.reshape(1, d))


def _local_kernel(u_ref, uh_ref, hc_ref, hch_ref, bg_ref, cg_ref, cgh_ref, wp_ref, ps_ref, cw_ref,
                  ya_ref, yc_ref, us_ref, zs_ref, *, bs, group):
    i = pl.program_id(0)
    has_hist = i > 0
    us_ref[0:HALO, :] = jnp.where(has_hist, uh_ref[...].astype(F32), 0.0)
    us_ref[HALO:, :] = u_ref[...].astype(F32)
    zs_ref[0:HALO, :] = jnp.where(has_hist, cgh_ref[...].astype(F32) * hch_ref[...].astype(F32), 0.0)
    zs_ref[HALO:, :] = cg_ref[...].astype(F32) * hc_ref[...].astype(F32)

    pos = i * bs + lax.broadcasted_iota(jnp.int32, (bs, group), 0)
    for g, w in enumerate(POOL_WINDOWS):
        cols = slice(g * group, (g + 1) * group)
        cur = us_ref[HALO:HALO + bs, cols]
        acc = cur
        for j in range(1, w):
            acc = acc + us_ref[HALO - j:HALO - j + bs, cols]
        count = jnp.minimum(pos + 1, w).astype(F32)
        diff = acc / count - cur
        y = jnp.dot(diff.astype(BF16), wp_ref[g], preferred_element_type=F32)
        ya_ref[:, cols] = (y * ps_ref[:, cols]).astype(ya_ref.dtype)

    conv = cw_ref[0:1, :] * zs_ref[HALO - 2:HALO - 2 + bs, :]
    for tap in range(1, CONV_TAPS):
        off = HALO - (CONV_TAPS - 1) + tap
        conv = conv + cw_ref[tap:tap + 1, :] * zs_ref[off:off + bs, :]
    yc_ref[...] = (bg_ref[...].astype(F32) * conv).astype(yc_ref.dtype)


def _local_mixers(proj, w_pool, pool_scale, conv_w, width, bs):
    s = proj.shape[0]
    group = width // len(POOL_WINDOWS)
    hb = bs // HALO

    def blk(c):
        return pl.BlockSpec((bs, width), lambda i, c=c: (i, c))

    def halo(c):
        return pl.BlockSpec((HALO, width), lambda i, c=c: (jnp.maximum(i * hb - 1, 0), c))

    out = pl.BlockSpec((bs, width), lambda i: (i, 0))
    vmem = 2 * (6 * _nbytes((bs, width), BF16) + _nbytes(w_pool.shape, BF16)) \
        + 2 * _nbytes((bs + HALO, width), F32) + 6 * _nbytes((bs, width), F32)
    return pl.pallas_call(
        functools.partial(_local_kernel, bs=bs, group=group),
        out_shape=(jax.ShapeDtypeStruct((s, width), BF16), jax.ShapeDtypeStruct((s, width), BF16)),
        grid=(s // bs,),
        in_specs=[blk(0), halo(0), blk(4), halo(4), blk(5), blk(6), halo(6),
                  pl.BlockSpec(w_pool.shape, lambda i: (0, 0, 0)),
                  pl.BlockSpec((1, width), lambda i: (0, 0)),
                  pl.BlockSpec((CONV_TAPS, width), lambda i: (0, 0))],
        out_specs=(out, out),
        scratch_shapes=[pltpu.VMEM((bs + HALO, width), F32), pltpu.VMEM((bs + HALO, width), F32)],
        compiler_params=_params(("arbitrary",), vmem),
        name="pool_conv_mixers",
    )(proj, proj, proj, proj, proj, proj, proj, w_pool, pool_scale.reshape(1, width), conv_w)


def _t5_bucket(dist):
    exact = N_BUCKETS // 2
    d = jnp.maximum(dist, 1).astype(F32)
    large = exact + (jnp.log(d / exact) / math.log(T5_MAX_DIST / exact) * (N_BUCKETS - exact)).astype(jnp.int32)
    return jnp.where(dist < exact, dist, jnp.minimum(large, N_BUCKETS - 1))


def _band_bias(t5_bias, window, dilation):
    steps = window // dilation
    qi = jnp.arange(BLOCK)[:, None]
    ki = jnp.arange(2 * BLOCK)[None, :]
    dist = qi + BLOCK - ki
    bucket = _t5_bucket(jnp.clip(dist, 0) * dilation)
    hit = bucket[..., None, None] == jnp.arange(N_BUCKETS)[:, None]
    bias = jnp.sum(jnp.where(hit, t5_bias.astype(F32)[None, None], 0.0), axis=2)
    bias = jnp.where(((dist >= 0) & (dist <= steps))[..., None], bias, NEG)
    return jnp.moveaxis(bias, -1, 0)


def _dilated_kernel(q_ref, kc_ref, kp_ref, vc_ref, vp_ref, bias_ref, o_ref,
                    q_scr, k_scr, v_scr, part_o, part_l, *, tile, branches):
    t = pl.program_id(1)
    q_scr[...] = q_ref[...].astype(F32)
    k_scr[0:tile, :] = kp_ref[...].astype(F32)
    k_scr[tile:, :] = kc_ref[...].astype(F32)
    v_scr[0:tile, :] = vp_ref[...].astype(F32)
    v_scr[tile:, :] = vc_ref[...].astype(F32)
    scale = HEAD_DIM ** -0.5
    before_start = (t == 0) & (lax.broadcasted_iota(jnp.int32, (BLOCK, 2 * BLOCK), 1) < BLOCK)

    def rows_of(start, size, d):
        return pl.ds(start, size) if d == 1 else pl.ds(start, size, stride=d)

    for idx, d in enumerate(branches):
        last = idx == len(branches) - 1
        for r in range(d):
            for b in range(tile // (d * BLOCK)):
                q_rows = rows_of(r + d * BLOCK * b, BLOCK, d)
                k_rows = rows_of(tile + r + d * BLOCK * (b - 1), 2 * BLOCK, d)
                s = lax.dot_general(q_scr[q_rows, :].astype(BF16), k_scr[k_rows, :].astype(BF16),
                                    (((1,), (1,)), ((), ())), preferred_element_type=F32)
                s = s * scale + bias_ref[idx]
                if b == 0:
                    s = jnp.where(before_start, NEG, s)
                m = jnp.max(s, axis=-1, keepdims=True)
                p = jnp.exp(s - m)
                den = jnp.sum(p, axis=-1, keepdims=True)
                out = jnp.dot(p.astype(BF16), v_scr[k_rows, :].astype(BF16), preferred_element_type=F32) / den
                lse = m + jnp.log(den)
                if not last:
                    part_o[idx, q_rows, :] = out
                    part_l[idx, q_rows, :] = jnp.broadcast_to(lse, (BLOCK, HEAD_DIM))
                else:
                    lses = [part_l[e, q_rows, :] for e in range(idx)] + [lse]
                    outs = [part_o[e, q_rows, :] for e in range(idx)] + [out]
                    top = functools.reduce(jnp.maximum, lses)
                    wts = [jnp.exp(x - top) for x in lses]
                    num = functools.reduce(lambda x, y: x + y, [w * o for w, o in zip(wts, outs)])
                    o_ref[q_rows, :] = (num / functools.reduce(lambda x, y: x + y, wts)).astype(o_ref.dtype)


def _dilated_attention(proj, t5_bias, width):
    s = proj.shape[0]
    heads = width // HEAD_DIM
    order = sorted(DIL_PATTERNS, key=lambda p: -p[1])
    branches = tuple(d for _, d in order)
    tile = max(branches) * BLOCK
    assert branches[-1] == 1 and s % tile == 0 and all(w // d == BLOCK and tile % (d * BLOCK) == 0 for w, d in order)
    bias = jnp.stack([_band_bias(t5_bias, w, d) for w, d in order], axis=1)

    def cur(c):
        return pl.BlockSpec((tile, HEAD_DIM), lambda h, t, c=c: (t, c * heads + h))

    def prev(c):
        return pl.BlockSpec((tile, HEAD_DIM), lambda h, t, c=c: (jnp.maximum(t - 1, 0), c * heads + h))

    nb = len(branches)
    vmem = 2 * (6 * _nbytes((tile, HEAD_DIM), BF16) + _nbytes(bias.shape[1:], F32)) \
        + (5 + 2 * (nb - 1)) * _nbytes((tile, HEAD_DIM), F32) + 64 * _nbytes((BLOCK, 2 * BLOCK), F32)
    return pl.pallas_call(
        functools.partial(_dilated_kernel, tile=tile, branches=branches),
        out_shape=jax.ShapeDtypeStruct((s, width), BF16),
        grid=(heads, s // tile),
        in_specs=[cur(1), cur(2), prev(2), cur(3), prev(3),
                  pl.BlockSpec((None,) + bias.shape[1:], lambda h, t: (h, 0, 0, 0))],
        out_specs=pl.BlockSpec((tile, HEAD_DIM), lambda h, t: (t, h)),
        scratch_shapes=[pltpu.VMEM((tile, HEAD_DIM), F32), pltpu.VMEM((2 * tile, HEAD_DIM), F32),
                        pltpu.VMEM((2 * tile, HEAD_DIM), F32),
                        pltpu.VMEM((nb - 1, tile, HEAD_DIM), F32), pltpu.VMEM((nb - 1, tile, HEAD_DIM), F32)],
        compiler_params=_params(("parallel", "arbitrary"), vmem),
        name="dilated_attention",
    )(proj, proj, proj, proj, proj, bias)


def _rms(x, g):
    return x * lax.rsqrt(jnp.mean(x * x, axis=-1, keepdims=True) + RMS_EPS) * g


def _rope_chunk(chunk, cos_ref, sin_ref):
    half = ROPE_DIM // 2
    swapped = pltpu.roll(chunk, HEAD_DIM - half, 1) + pltpu.roll(chunk, half, 1)
    return chunk * cos_ref[...] + swapped * sin_ref[...]


def _mla_q_kernel(cq_ref, g_ref, w_ref, cos_ref, sin_ref, qt_ref, *, heads, scale):
    qn = _rms(cq_ref[...], g_ref[...]).astype(BF16)
    q = jnp.dot(qn, w_ref[...], preferred_element_type=F32)
    for h in range(heads):
        nope = q[:, h * MLA_QK_PAD:h * MLA_QK_PAD + HEAD_DIM]
        rope = _rope_chunk(q[:, h * MLA_QK_PAD + HEAD_DIM:(h + 1) * MLA_QK_PAD], cos_ref, sin_ref)
        qt_ref[h * MLA_QK_PAD:h * MLA_QK_PAD + HEAD_DIM, :] = (nope * scale).T.astype(qt_ref.dtype)
        qt_ref[h * MLA_QK_PAD + HEAD_DIM:(h + 1) * MLA_QK_PAD, :] = (rope * scale).T.astype(qt_ref.dtype)


def _mla_kv_kernel(ckv_ref, kr_ref, g_ref, w_ref, cos_ref, sin_ref, k_ref, vt_ref, *, heads, bk):
    kvn = _rms(ckv_ref[...], g_ref[...]).astype(BF16)
    kv = jnp.dot(kvn, w_ref[...], preferred_element_type=F32)
    k_rope = _rope_chunk(kr_ref[...], cos_ref, sin_ref).astype(k_ref.dtype)
    for h in range(heads):
        k_ref[:, h * MLA_QK_PAD:h * MLA_QK_PAD + HEAD_DIM] = \
            kv[:, h * HEAD_DIM:(h + 1) * HEAD_DIM].astype(k_ref.dtype)
        k_ref[:, h * MLA_QK_PAD + HEAD_DIM:(h + 1) * MLA_QK_PAD] = k_rope
        for c in range(vt_ref.shape[1]):
            v = kv[c * bk:(c + 1) * bk, (heads + h) * HEAD_DIM:(heads + h + 1) * HEAD_DIM]
            vt_ref[h, c] = v.T.astype(vt_ref.dtype)


def _mla_flash_kernel(qt_ref, k_ref, vt_ref, o_ref, m_scr, l_scr, acc_scr, s_scr, *, bk):
    i = pl.program_id(1)
    m_scr[...] = jnp.full(m_scr.shape, NEG, F32)
    l_scr[...] = jnp.zeros(l_scr.shape, F32)
    acc_scr[...] = jnp.zeros(acc_scr.shape, F32)
    qts = [qt_ref[:, c * bk:(c + 1) * bk] for c in range(2)]

    def keys(j):
        return k_ref[pl.ds(pl.multiple_of(j * bk, bk), bk), :]

    def logits(c, k_blk):
        return jnp.dot(k_blk, qts[c], preferred_element_type=F32)

    def consume(c, s, vt_blk, diagonal):
        if diagonal:
            key = lax.broadcasted_iota(jnp.int32, (bk, bk), 0)
            qry = lax.broadcasted_iota(jnp.int32, (bk, bk), 1)
            s = jnp.where(key <= qry, s, NEG)
        m_prev = m_scr[c]
        m_new = jnp.maximum(m_prev, jnp.max(s, axis=0, keepdims=True))
        alpha = jnp.exp2(m_prev - m_new)
        p = jnp.exp2(s - m_new)
        l_scr[c] = alpha * l_scr[c] + jnp.sum(p, axis=0, keepdims=True)
        acc_scr[c] = alpha * acc_scr[c] + jnp.dot(vt_blk, p.astype(BF16), preferred_element_type=F32)
        m_scr[c] = m_new

    k_first = keys(0)
    for c in range(2):
        s_scr[c] = logits(c, k_first)

    def body(j, carry):
        k_next = keys(j + 1)
        vt_blk = vt_ref[j]
        for c in range(2):
            s = s_scr[c]
            s_scr[c] = logits(c, k_next)
            consume(c, s, vt_blk, False)
        return carry

    first_diag = 2 * i
    lax.fori_loop(0, first_diag, body, 0)
    s_last = logits(1, keys(first_diag + 1))
    consume(0, s_scr[0], vt_ref[first_diag], True)
    consume(1, s_scr[1], vt_ref[first_diag], False)
    consume(1, s_last, vt_ref[first_diag + 1], True)
    for c in range(2):
        o_ref[c * bk:(c + 1) * bk, :] = (acc_scr[c] / l_scr[c]).T.astype(o_ref.dtype)


def _rope_tables(s):
    half = ROPE_DIM // 2
    pos = jnp.arange(s, dtype=F32)
    inv = ROPE_THETA ** (-jnp.arange(0, ROPE_DIM, 2, dtype=F32) / ROPE_DIM)
    ang = pos[:, None] * inv[None, :]
    cos, sin = jnp.cos(ang), jnp.sin(ang)
    zeros = jnp.zeros((s, HEAD_DIM - 2 * half), F32)
    return jnp.concatenate([cos, cos, zeros], axis=1), jnp.concatenate([-sin, sin, zeros], axis=1)


def _mla(lat, q_rank, kv_rank, q_norm_g, w_uq_p, kv_norm_g, w_ukv_p, cos_t, sin_t, heads):
    s = lat.shape[0]
    bm = _pick(s, (1024, 512, 256, 128))
    bq = _pick(s, (512, 256, 128))
    assert (q_rank - kv_rank) % HEAD_DIM == 0 and kv_rank % HEAD_DIM == 0 and bm % bq == 0
    tab = pl.BlockSpec((bm, HEAD_DIM), lambda i: (i, 0))
    qk_width = heads * MLA_QK_PAD
    v_width = heads * HEAD_DIM
    scale = (HEAD_DIM + ROPE_DIM) ** -0.5 * math.log2(math.e)

    vmem_q = 2 * (_nbytes((bm, q_rank), F32) + _nbytes(w_uq_p.shape, BF16) + 2 * _nbytes((bm, HEAD_DIM), F32)
                  + _nbytes((bm, qk_width), BF16)) + 3 * _nbytes((bm, qk_width), F32)
    qt = pl.pallas_call(
        functools.partial(_mla_q_kernel, heads=heads, scale=scale),
        out_shape=jax.ShapeDtypeStruct((qk_width, s), BF16),
        grid=(s // bm,),
        in_specs=[pl.BlockSpec((bm, q_rank), lambda i: (i, 1)),
                  pl.BlockSpec((1, q_rank), lambda i: (0, 0)),
                  pl.BlockSpec(w_uq_p.shape, lambda i: (0, 0)), tab, tab],
        out_specs=pl.BlockSpec((qk_width, bm), lambda i: (0, i)),
        compiler_params=_params(("parallel",), vmem_q),
        name="mla_q_up",
    )(lat, q_norm_g.reshape(1, q_rank), w_uq_p, cos_t, sin_t)

    vmem_kv = 2 * (_nbytes((bm, kv_rank), F32) + _nbytes(w_ukv_p.shape, BF16) + 3 * _nbytes((bm, HEAD_DIM), F32)
                   + _nbytes((bm, qk_width), BF16) + _nbytes((bm, v_width), BF16)) \
        + 3 * _nbytes((bm, 2 * v_width), F32)
    k, vt = pl.pallas_call(
        functools.partial(_mla_kv_kernel, heads=heads, bk=bq),
        out_shape=(jax.ShapeDtypeStruct((s, qk_width), BF16),
                   jax.ShapeDtypeStruct((heads, s // bq, HEAD_DIM, bq), BF16)),
        grid=(s // bm,),
        in_specs=[pl.BlockSpec((bm, kv_rank), lambda i: (i, 0)),
                  pl.BlockSpec((bm, HEAD_DIM), lambda i: (i, kv_rank // HEAD_DIM)),
                  pl.BlockSpec((1, kv_rank), lambda i: (0, 0)),
                  pl.BlockSpec(w_ukv_p.shape, lambda i: (0, 0)), tab, tab],
        out_specs=(pl.BlockSpec((bm, qk_width), lambda i: (i, 0)),
                   pl.BlockSpec((heads, bm // bq, HEAD_DIM, bq), lambda i: (0, i, 0, 0))),
        compiler_params=_params(("parallel",), vmem_kv),
        name="mla_kv_up",
    )(lat, lat, kv_norm_g.reshape(1, kv_rank), w_ukv_p, cos_t, sin_t)

    rows = 2 * bq
    assert s % rows == 0
    vmem_a = 2 * (_nbytes((MLA_QK_PAD, rows), BF16) + _nbytes((s, MLA_QK_PAD), BF16) + _nbytes((s, HEAD_DIM), BF16)
                  + _nbytes((rows, HEAD_DIM), BF16)) + 3 * _nbytes((HEAD_DIM, rows), F32) \
        + 12 * _nbytes((bq, bq), F32)
    return pl.pallas_call(
        functools.partial(_mla_flash_kernel, bk=bq),
        out_shape=jax.ShapeDtypeStruct((s, v_width), BF16),
        grid=(heads, s // rows),
        in_specs=[pl.BlockSpec((MLA_QK_PAD, rows), lambda h, i: (h, i)),
                  pl.BlockSpec((s, MLA_QK_PAD), lambda h, i: (0, h)),
                  pl.BlockSpec((None, s // bq, HEAD_DIM, bq), lambda h, i: (h, 0, 0, 0))],
        out_specs=pl.BlockSpec((rows, HEAD_DIM), lambda h, i: (i, h)),
        scratch_shapes=[pltpu.VMEM((2, 1, bq), F32), pltpu.VMEM((2, 1, bq), F32),
                        pltpu.VMEM((2, HEAD_DIM, bq), F32), pltpu.VMEM((2, bq, bq), F32)],
        compiler_params=_params(("parallel", "arbitrary"), vmem_a),
        name="mla_flash_attention",
    )(qt, k, vt)


def _pad_cols(w, width):
    return jnp.pad(w, ((0, 0), (0, width - w.shape[1])))


def _prep_layer(l, d, w_in, w_uq, w_ukv, w_gate, w_up, w_down, heads, f_pad):
    width = d // 4
    q_rank = w_uq.shape[1]
    kv_rank = w_ukv.shape[1]
    n_main = 7 * width
    w = w_in[l]
    w_main = w[:, :n_main].astype(BF16)
    cq = w[:, n_main:n_main + q_rank]
    ckv = w[:, n_main + q_rank:n_main + q_rank + kv_rank]
    kr = w[:, n_main + q_rank + kv_rank:]
    w_lat = jnp.concatenate([ckv, _pad_cols(kr, q_rank - kv_rank), cq], axis=1).astype(BF16)
    uq = w_uq[l].reshape(q_rank, heads, HEAD_DIM + ROPE_DIM)
    uq = jnp.pad(uq, ((0, 0), (0, 0), (0, MLA_QK_PAD - HEAD_DIM - ROPE_DIM))).reshape(q_rank, heads * MLA_QK_PAD)
    ukv = w_ukv[l].reshape(kv_rank, heads, 2, HEAD_DIM).transpose(0, 2, 1, 3).reshape(kv_rank, 2 * heads * HEAD_DIM)
    f = w_gate.shape[-1]
    return dict(
        w_main=w_main, w_lat=w_lat, w_uq=uq.astype(BF16), w_ukv=ukv.astype(BF16),
        w_gate=_pad_cols(w_gate[l], f_pad).astype(BF16), w_up=_pad_cols(w_up[l], f_pad).astype(BF16),
        w_down=jnp.pad(w_down[l], ((0, f_pad - f), (0, 0))).astype(BF16),
    )


def kernel(x, w_in, w_pool, pool_scale, t5_bias, conv_w, q_norm_g, w_uq, kv_norm_g, w_ukv, w_out, ln1_g, ln1_b,
           w_gate, w_up, w_down, ln2_g, ln2_b):
    batch, s, d = x.shape
    depth = w_in.shape[0]
    width = d // 4
    heads = (d - 3 * width) // HEAD_DIM
    q_rank = w_uq.shape[1]
    kv_rank = w_ukv.shape[1]
    f = w_gate.shape[-1]
    f_pad = -(-f // 1024) * 1024
    alpha = (2 * depth) ** 0.25
    assert batch == 1 and width % (len(POOL_WINDOWS) * HEAD_DIM) == 0

    bm = _pick(s, (1024, 512, 256, 128))
    cos_t, sin_t = _rope_tables(s)
    xf = x.reshape(s, d)
    xb = xf.astype(BF16)
    for l in range(depth):
        p = _prep_layer(l, d, w_in, w_uq, w_ukv, w_gate, w_up, w_down, heads, f_pad)
        proj = _matmul(xb, p["w_main"], BF16, bm, _pick(7 * width, (1024, 896, 512, 256, 128)), name="in_proj_main")
        lat = _matmul(xb, p["w_lat"], F32, bm, _pick(2 * q_rank, (768, 512, 384, 256, 128)), name="in_proj_latent")
        y_a, y_c = _local_mixers(proj, w_pool[l].astype(BF16), pool_scale[l], conv_w[l], width,
                                 _pick(s, (512, 256, 128)))
        y_b = _dilated_attention(proj, t5_bias, width)
        y_d = _mla(lat, q_rank, kv_rank, q_norm_g[l], p["w_uq"], kv_norm_g[l], p["w_ukv"], cos_t, sin_t, heads)
        mixed = jnp.concatenate([y_a, y_b, y_c, y_d], axis=-1)
        y = _matmul(mixed, w_out[l].astype(BF16), F32, bm, _pick(d, (1024, 512, 256, 128)), name="out_proj")
        xf, xb = _add_ln(y, xf, ln1_g[l], ln1_b[l], alpha, _pick(s, (256, 128)))
        hid = _gate_up(xb, p["w_gate"], p["w_up"], bm, _pick(f_pad, (512, 256, 128)))
        y = _matmul(hid, p["w_down"], F32, bm, _pick(d, (1024, 512, 256, 128)),
                    bk=_pick(f_pad, (2816, 2048, 1024, 512)), name="ffn_down")
        xf, xb = _add_ln(y, xf, ln2_g[l], ln2_b[l], alpha, _pick(s, (256, 128)))
    return xf.reshape(batch, s, d)
```

```python
import functools
import math

import jax
import jax.numpy as jnp
from jax import lax
from jax.experimental import pallas as pl
from jax.experimental.pallas import tpu as pltpu

BLOCK = 128
HEAD_DIM = 128
ROPE_DIM = 64
MLA_QK_PAD = 256
ONES_ROWS = 16
POOL_WINDOWS = (2, 4, 8, 16)
DIL_PATTERNS = ((128, 1), (512, 4), (2048, 16))
N_BUCKETS = 32
T5_MAX_DIST = 2048
CONV_TAPS = 3
ROPE_THETA = 10000.0
LN_EPS = 1e-5
RMS_EPS = 1e-6
HALO = 16
NEG = -1e30

V7X_VMEM_BUDGET = 60 * 1024 * 1024
VMEM_SLACK = 8 * 1024 * 1024

F32 = jnp.float32
BF16 = jnp.bfloat16


def _params(sem, vmem_bytes):
    return pltpu.CompilerParams(dimension_semantics=sem,
                                vmem_limit_bytes=int(min(vmem_bytes + VMEM_SLACK, V7X_VMEM_BUDGET)))


def _nbytes(shape, dtype):
    return math.prod(shape) * jnp.dtype(dtype).itemsize


def _pick(n, prefs):
    for p in prefs:
        if n % p == 0:
            return p
    return n


def _mm_kernel(*refs, nk):
    *a_refs, w_ref, o_ref = refs
    a = a_refs[0][...] if len(a_refs) == 1 else jnp.concatenate([r[...] for r in a_refs], axis=1)
    part = jnp.dot(a, w_ref[...], preferred_element_type=F32)
    if nk == 1:
        o_ref[...] = part.astype(o_ref.dtype)
    else:
        k = pl.program_id(2)

        @pl.when(k == 0)
        def _():
            o_ref[...] = part

        @pl.when(k > 0)
        def _():
            o_ref[...] += part


def _matmul(a, w, out_dtype, bm, bn, bk=None, name="matmul"):
    parts = list(a) if isinstance(a, (list, tuple)) else [a]
    m = parts[0].shape[0]
    kdim, n = w.shape
    bk = kdim if bk is None else bk
    nk = kdim // bk
    assert m % bm == 0 and n % bn == 0 and kdim % bk == 0 and sum(p.shape[1] for p in parts) == kdim
    assert nk == 1 or (out_dtype == F32 and len(parts) == 1)
    if len(parts) == 1:
        a_specs = [pl.BlockSpec((bm, bk), lambda i, j, k: (i, k))]
    else:
        a_specs = [pl.BlockSpec((bm, p.shape[1]), lambda i, j, k: (i, 0)) for p in parts]
    vmem = 2 * (_nbytes((bm, bk), parts[0].dtype) + _nbytes((bk, bn), w.dtype) + _nbytes((bm, bn), out_dtype)) \
        + 2 * _nbytes((bm, bn), F32) + (len(parts) > 1) * _nbytes((bm, bk), parts[0].dtype)
    return pl.pallas_call(
        functools.partial(_mm_kernel, nk=nk),
        out_shape=jax.ShapeDtypeStruct((m, n), out_dtype),
        grid=(m // bm, n // bn, nk),
        in_specs=a_specs + [pl.BlockSpec((bk, bn), lambda i, j, k: (k, j))],
        out_specs=pl.BlockSpec((bm, bn), lambda i, j, k: (i, j)),
        compiler_params=_params(("parallel", "parallel", "arbitrary"), vmem),
        name=name,
    )(*parts, w)


def _gate_up_kernel(a_ref, wg_ref, wu_ref, o_ref):
    a = a_ref[...]
    g = jnp.dot(a, wg_ref[...], preferred_element_type=F32)
    u = jnp.dot(a, wu_ref[...], preferred_element_type=F32)
    o_ref[...] = (g * (1.0 / (1.0 + jnp.exp(-g))) * u).astype(o_ref.dtype)


def _gate_up(a, wg, wu, bm, bn):
    m, kdim = a.shape
    n = wg.shape[1]
    vmem = 2 * (_nbytes((bm, kdim), a.dtype) + 2 * _nbytes((kdim, bn), wg.dtype) + _nbytes((bm, bn), BF16)) \
        + 4 * _nbytes((bm, bn), F32)
    return pl.pallas_call(
        _gate_up_kernel,
        out_shape=jax.ShapeDtypeStruct((m, n), BF16),
        grid=(m // bm, n // bn),
        in_specs=[pl.BlockSpec((bm, kdim), lambda i, j: (i, 0)),
                  pl.BlockSpec((kdim, bn), lambda i, j: (0, j)),
                  pl.BlockSpec((kdim, bn), lambda i, j: (0, j))],
        out_specs=pl.BlockSpec((bm, bn), lambda i, j: (i, j)),
        compiler_params=_params(("parallel", "parallel"), vmem),
        name="ffn_gate_up",
    )(a, wg, wu)


def _add_ln_kernel(y_ref, x_ref, g_ref, b_ref, of_ref, ob_ref, *, alpha):
    z = alpha * x_ref[...] + y_ref[...]
    mu = jnp.mean(z, axis=-1, keepdims=True)
    zc = z - mu
    var = jnp.mean(zc * zc, axis=-1, keepdims=True)
    out = zc * lax.rsqrt(var + LN_EPS) * g_ref[...] + b_ref[...]
    of_ref[...] = out
    ob_ref[...] = out.astype(BF16)


def _add_ln(y, x, g, b, alpha, br):
    s, d = x.shape
    row = pl.BlockSpec((br, d), lambda i: (i, 0))
    vec = pl.BlockSpec((1, d), lambda i: (0, 0))
    vmem = 2 * (3 * _nbytes((br, d), F32) + _nbytes((br, d), BF16)) + 4 * _nbytes((br, d), F32)
    return pl.pallas_call(
        functools.partial(_add_ln_kernel, alpha=alpha),
        out_shape=(jax.ShapeDtypeStruct((s, d), F32), jax.ShapeDtypeStruct((s, d), BF16)),
        grid=(s // br,),
        in_specs=[row, row, vec, vec],
        out_specs=(row, row),
        compiler_params=_params(("parallel",), vmem),
        name="add_layernorm",
    )(y, x, g.reshape(1, d), b.reshape(1, d))


def _local_kernel(u_ref, uh_ref, hc_ref, hch_ref, bg_ref, cg_ref, cgh_ref, wp_ref, ps_ref, cw_ref,
                  ya_ref, yc_ref, us_ref, zs_ref, *, bs, group):
    i = pl.program_id(0)
    has_hist = i > 0
    us_ref[0:HALO, :] = jnp.where(has_hist, uh_ref[...].astype(F32), 0.0)
    us_ref[HALO:, :] = u_ref[...].astype(F32)
    zs_ref[0:HALO, :] = jnp.where(has_hist, cgh_ref[...].astype(F32) * hch_ref[...].astype(F32), 0.0)
    zs_ref[HALO:, :] = cg_ref[...].astype(F32) * hc_ref[...].astype(F32)

    pos = i * bs + lax.broadcasted_iota(jnp.int32, (bs, group), 0)
    for g, w in enumerate(POOL_WINDOWS):
        cols = slice(g * group, (g + 1) * group)
        cur = us_ref[HALO:HALO + bs, cols]
        acc = cur
        for j in range(1, w):
            acc = acc + us_ref[HALO - j:HALO - j + bs, cols]
        count = jnp.minimum(pos + 1, w).astype(F32)
        diff = acc / count - cur
        y = jnp.dot(diff.astype(BF16), wp_ref[g], preferred_element_type=F32)
        ya_ref[:, cols] = (y * ps_ref[:, cols]).astype(ya_ref.dtype)

    conv = cw_ref[0:1, :] * zs_ref[HALO - 2:HALO - 2 + bs, :]
    for tap in range(1, CONV_TAPS):
        off = HALO - (CONV_TAPS - 1) + tap
        conv = conv + cw_ref[tap:tap + 1, :] * zs_ref[off:off + bs, :]
    yc_ref[...] = (bg_ref[...].astype(F32) * conv).astype(yc_ref.dtype)


def _local_mixers(proj, w_pool, pool_scale, conv_w, width, bs):
    s = proj.shape[0]
    group = width // len(POOL_WINDOWS)
    hb = bs // HALO

    def blk(c):
        return pl.BlockSpec((bs, width), lambda i, c=c: (i, c))

    def halo(c):
        return pl.BlockSpec((HALO, width), lambda i, c=c: (jnp.maximum(i * hb - 1, 0), c))

    out = pl.BlockSpec((bs, width), lambda i: (i, 0))
    vmem = 2 * (6 * _nbytes((bs, width), BF16) + _nbytes(w_pool.shape, BF16)) \
        + 2 * _nbytes((bs + HALO, width), F32) + 6 * _nbytes((bs, width), F32)
    return pl.pallas_call(
        functools.partial(_local_kernel, bs=bs, group=group),
        out_shape=(jax.ShapeDtypeStruct((s, width), BF16), jax.ShapeDtypeStruct((s, width), BF16)),
        grid=(s // bs,),
        in_specs=[blk(0), halo(0), blk(4), halo(4), blk(5), blk(6), halo(6),
                  pl.BlockSpec(w_pool.shape, lambda i: (0, 0, 0)),
                  pl.BlockSpec((1, width), lambda i: (0, 0)),
                  pl.BlockSpec((CONV_TAPS, width), lambda i: (0, 0))],
        out_specs=(out, out),
        scratch_shapes=[pltpu.VMEM((bs + HALO, width), F32), pltpu.VMEM((bs + HALO, width), F32)],
        compiler_params=_params(("arbitrary",), vmem),
        name="pool_conv_mixers",
    )(proj, proj, proj, proj, proj, proj, proj, w_pool, pool_scale.reshape(1, width), conv_w)


def _t5_bucket(dist):
    exact = N_BUCKETS // 2
    d = jnp.maximum(dist, 1).astype(F32)
    large = exact + (jnp.log(d / exact) / math.log(T5_MAX_DIST / exact) * (N_BUCKETS - exact)).astype(jnp.int32)
    return jnp.where(dist < exact, dist, jnp.minimum(large, N_BUCKETS - 1))


def _band_bias(t5_bias, window, dilation):
    steps = window // dilation
    qi = jnp.arange(BLOCK)[:, None]
    ki = jnp.arange(2 * BLOCK)[None, :]
    dist = qi + BLOCK - ki
    bucket = _t5_bucket(jnp.clip(dist, 0) * dilation)
    hit = bucket[..., None, None] == jnp.arange(N_BUCKETS)[:, None]
    bias = jnp.sum(jnp.where(hit, t5_bias.astype(F32)[None, None], 0.0), axis=2)
    bias = jnp.where(((dist >= 0) & (dist <= steps))[..., None], bias, NEG)
    return jnp.moveaxis(bias, -1, 0)


def _dilated_kernel(q_ref, kc_ref, kp_ref, vc_ref, vp_ref, bias_ref, o_ref,
                    q_scr, k_scr, v_scr, part_o, part_l, *, tile, branches):
    t = pl.program_id(1)
    q_scr[...] = q_ref[...].astype(F32)
    k_scr[0:tile, :] = kp_ref[...].astype(F32)
    k_scr[tile:, :] = kc_ref[...].astype(F32)
    v_scr[0:tile, :] = vp_ref[...].astype(F32)
    v_scr[tile:, :] = vc_ref[...].astype(F32)
    scale = HEAD_DIM ** -0.5
    before_start = (t == 0) & (lax.broadcasted_iota(jnp.int32, (BLOCK, 2 * BLOCK), 1) < BLOCK)

    def rows_of(start, size, d):
        return pl.ds(start, size) if d == 1 else pl.ds(start, size, stride=d)

    for idx, d in enumerate(branches):
        last = idx == len(branches) - 1
        for r in range(d):
            for b in range(tile // (d * BLOCK)):
                q_rows = rows_of(r + d * BLOCK * b, BLOCK, d)
                k_rows = rows_of(tile + r + d * BLOCK * (b - 1), 2 * BLOCK, d)
                s = lax.dot_general(q_scr[q_rows, :].astype(BF16), k_scr[k_rows, :].astype(BF16),
                                    (((1,), (1,)), ((), ())), preferred_element_type=F32)
                s = s * scale + bias_ref[idx]
                if b == 0:
                    s = jnp.where(before_start, NEG, s)
                m = jnp.max(s, axis=-1, keepdims=True)
                p = jnp.exp(s - m)
                den = jnp.sum(p, axis=-1, keepdims=True)
                out = jnp.dot(p.astype(BF16), v_scr[k_rows, :].astype(BF16), preferred_element_type=F32) / den
                lse = m + jnp.log(den)
                if not last:
                    part_o[idx, q_rows, :] = out
                    part_l[idx, q_rows, :] = jnp.broadcast_to(lse, (BLOCK, HEAD_DIM))
                else:
                    lses = [part_l[e, q_rows, :] for e in range(idx)] + [lse]
                    outs = [part_o[e, q_rows, :] for e in range(idx)] + [out]
                    top = functools.reduce(jnp.maximum, lses)
                    wts = [jnp.exp(x - top) for x in lses]
                    num = functools.reduce(lambda x, y: x + y, [w * o for w, o in zip(wts, outs)])
                    o_ref[q_rows, :] = (num / functools.reduce(lambda x, y: x + y, wts)).astype(o_ref.dtype)


def _dilated_attention(proj, t5_bias, width):
    s = proj.shape[0]
    heads = width // HEAD_DIM
    order = sorted(DIL_PATTERNS, key=lambda p: -p[1])
    branches = tuple(d for _, d in order)
    tile = max(branches) * BLOCK
    assert branches[-1] == 1 and s % tile == 0 and all(w // d == BLOCK and tile % (d * BLOCK) == 0 for w, d in order)
    bias = jnp.stack([_band_bias(t5_bias, w, d) for w, d in order], axis=1)

    def cur(c):
        return pl.BlockSpec((tile, HEAD_DIM), lambda h, t, c=c: (t, c * heads + h))

    def prev(c):
        return pl.BlockSpec((tile, HEAD_DIM), lambda h, t, c=c: (jnp.maximum(t - 1, 0), c * heads + h))

    nb = len(branches)
    vmem = 2 * (6 * _nbytes((tile, HEAD_DIM), BF16) + _nbytes(bias.shape[1:], F32)) \
        + (5 + 2 * (nb - 1)) * _nbytes((tile, HEAD_DIM), F32) + 64 * _nbytes((BLOCK, 2 * BLOCK), F32)
    return pl.pallas_call(
        functools.partial(_dilated_kernel, tile=tile, branches=branches),
        out_shape=jax.ShapeDtypeStruct((s, width), BF16),
        grid=(heads, s // tile),
        in_specs=[cur(1), cur(2), prev(2), cur(3), prev(3),
                  pl.BlockSpec((None,) + bias.shape[1:], lambda h, t: (h, 0, 0, 0))],
        out_specs=pl.BlockSpec((tile, HEAD_DIM), lambda h, t: (t, h)),
        scratch_shapes=[pltpu.VMEM((tile, HEAD_DIM), F32), pltpu.VMEM((2 * tile, HEAD_DIM), F32),
                        pltpu.VMEM((2 * tile, HEAD_DIM), F32),
                        pltpu.VMEM((nb - 1, tile, HEAD_DIM), F32), pltpu.VMEM((nb - 1, tile, HEAD_DIM), F32)],
        compiler_params=_params(("parallel", "arbitrary"), vmem),
        name="dilated_attention",
    )(proj, proj, proj, proj, proj, bias)


def _rms(x, g):
    return x * lax.rsqrt(jnp.mean(x * x, axis=-1, keepdims=True) + RMS_EPS) * g


def _rope_chunk(chunk, cos_ref, sin_ref):
    half = ROPE_DIM // 2
    swapped = pltpu.roll(chunk, HEAD_DIM - half, 1) + pltpu.roll(chunk, half, 1)
    return chunk * cos_ref[...] + swapped * sin_ref[...]


def _mla_q_kernel(cq_ref, g_ref, w_ref, cos_ref, sin_ref, qt_ref, *, heads, scale):
    qn = _rms(cq_ref[...], g_ref[...]).astype(BF16)
    q = jnp.dot(qn, w_ref[...], preferred_element_type=F32)
    for h in range(heads):
        nope = q[:, h * MLA_QK_PAD:h * MLA_QK_PAD + HEAD_DIM]
        rope = _rope_chunk(q[:, h * MLA_QK_PAD + HEAD_DIM:(h + 1) * MLA_QK_PAD], cos_ref, sin_ref)
        qt_ref[h * MLA_QK_PAD:h * MLA_QK_PAD + HEAD_DIM, :] = (nope * scale).T.astype(qt_ref.dtype)
        qt_ref[h * MLA_QK_PAD + HEAD_DIM:(h + 1) * MLA_QK_PAD, :] = (rope * scale).T.astype(qt_ref.dtype)


def _mla_kv_kernel(ckv_ref, kr_ref, g_ref, w_ref, cos_ref, sin_ref, k_ref, vt_ref, *, heads, bk):
    kvn = _rms(ckv_ref[...], g_ref[...]).astype(BF16)
    kv = jnp.dot(kvn, w_ref[...], preferred_element_type=F32)
    k_rope = _rope_chunk(kr_ref[...], cos_ref, sin_ref).astype(k_ref.dtype)
    for h in range(heads):
        k_ref[:, h * MLA_QK_PAD:h * MLA_QK_PAD + HEAD_DIM] = \
            kv[:, h * HEAD_DIM:(h + 1) * HEAD_DIM].astype(k_ref.dtype)
        k_ref[:, h * MLA_QK_PAD + HEAD_DIM:(h + 1) * MLA_QK_PAD] = k_rope
        for c in range(vt_ref.shape[1]):
            v = kv[c * bk:(c + 1) * bk, (heads + h) * HEAD_DIM:(heads + h + 1) * HEAD_DIM]
            vt_ref[h, c, 0:HEAD_DIM, :] = v.T.astype(vt_ref.dtype)
            vt_ref[h, c, HEAD_DIM:, :] = jnp.ones((ONES_ROWS, bk), vt_ref.dtype)


def _mla_flash_kernel(qt_ref, k_ref, vt_ref, o_ref, m_scr, acc_scr, s_scr, smax_scr, *, bk):
    i = pl.program_id(1)
    m_scr[...] = jnp.full(m_scr.shape, NEG, F32)
    acc_scr[...] = jnp.zeros(acc_scr.shape, F32)
    qts = [qt_ref[:, c * bk:(c + 1) * bk] for c in range(2)]

    def keys(j):
        return k_ref[pl.ds(pl.multiple_of(j * bk, bk), bk), :]

    def logits(c, k_blk):
        s = jnp.dot(k_blk, qts[c], preferred_element_type=F32)
        return s, jnp.max(s, axis=0, keepdims=True)

    def consume(c, s, s_max, vt_blk, diagonal):
        if diagonal:
            key = lax.broadcasted_iota(jnp.int32, (bk, bk), 0)
            qry = lax.broadcasted_iota(jnp.int32, (bk, bk), 1)
            s = jnp.where(key <= qry, s, NEG)
            s_max = jnp.max(s, axis=0, keepdims=True)
        m_prev = m_scr[c]
        m_new = jnp.maximum(m_prev, s_max)
        alpha = jnp.exp2(m_prev - m_new)
        p = jnp.exp2(s - m_new).astype(BF16)
        acc_scr[c] = alpha * acc_scr[c] + jnp.dot(vt_blk, p, preferred_element_type=F32)
        m_scr[c] = m_new

    k_first = keys(0)
    for c in range(2):
        s_scr[c], smax_scr[c] = logits(c, k_first)

    def body(j, carry):
        k_next = keys(j + 1)
        vt_blk = vt_ref[j]
        for c in range(2):
            s, s_max = s_scr[c], smax_scr[c]
            s_scr[c], smax_scr[c] = logits(c, k_next)
            consume(c, s, s_max, vt_blk, False)
        return carry

    first_diag = 2 * i
    lax.fori_loop(0, first_diag, body, 0)
    s_last, _ = logits(1, keys(first_diag + 1))
    consume(0, s_scr[0], None, vt_ref[first_diag], True)
    consume(1, s_scr[1], smax_scr[1], vt_ref[first_diag], False)
    consume(1, s_last, None, vt_ref[first_diag + 1], True)
    for c in range(2):
        acc = acc_scr[c]
        o_ref[c * bk:(c + 1) * bk, :] = (acc[0:HEAD_DIM] / acc[HEAD_DIM:HEAD_DIM + 1]).T.astype(o_ref.dtype)


def _rope_tables(s):
    half = ROPE_DIM // 2
    pos = jnp.arange(s, dtype=F32)
    inv = ROPE_THETA ** (-jnp.arange(0, ROPE_DIM, 2, dtype=F32) / ROPE_DIM)
    ang = pos[:, None] * inv[None, :]
    cos, sin = jnp.cos(ang), jnp.sin(ang)
    zeros = jnp.zeros((s, HEAD_DIM - 2 * half), F32)
    return jnp.concatenate([cos, cos, zeros], axis=1), jnp.concatenate([-sin, sin, zeros], axis=1)


def _mla(lat, q_rank, kv_rank, q_norm_g, w_uq_p, kv_norm_g, w_ukv_p, cos_t, sin_t, heads):
    s = lat.shape[0]
    bm = _pick(s, (1024, 512, 256, 128))
    bq = _pick(s, (512, 256, 128))
    assert (q_rank - kv_rank) % HEAD_DIM == 0 and kv_rank % HEAD_DIM == 0 and bm % bq == 0
    tab = pl.BlockSpec((bm, HEAD_DIM), lambda i: (i, 0))
    qk_width = heads * MLA_QK_PAD
    v_width = heads * HEAD_DIM
    vt_rows = HEAD_DIM + ONES_ROWS
    scale = (HEAD_DIM + ROPE_DIM) ** -0.5 * math.log2(math.e)

    vmem_q = 2 * (_nbytes((bm, q_rank), F32) + _nbytes(w_uq_p.shape, BF16) + 2 * _nbytes((bm, HEAD_DIM), F32)
                  + _nbytes((bm, qk_width), BF16)) + 3 * _nbytes((bm, qk_width), F32)
    qt = pl.pallas_call(
        functools.partial(_mla_q_kernel, heads=heads, scale=scale),
        out_shape=jax.ShapeDtypeStruct((qk_width, s), BF16),
        grid=(s // bm,),
        in_specs=[pl.BlockSpec((bm, q_rank), lambda i: (i, 1)),
                  pl.BlockSpec((1, q_rank), lambda i: (0, 0)),
                  pl.BlockSpec(w_uq_p.shape, lambda i: (0, 0)), tab, tab],
        out_specs=pl.BlockSpec((qk_width, bm), lambda i: (0, i)),
        compiler_params=_params(("parallel",), vmem_q),
        name="mla_q_up",
    )(lat, q_norm_g.reshape(1, q_rank), w_uq_p, cos_t, sin_t)

    vmem_kv = 2 * (_nbytes((bm, kv_rank), F32) + _nbytes(w_ukv_p.shape, BF16) + 3 * _nbytes((bm, HEAD_DIM), F32)
                   + _nbytes((bm, qk_width), BF16) + _nbytes((bm, v_width), BF16)) \
        + 3 * _nbytes((bm, 2 * v_width), F32)
    k, vt = pl.pallas_call(
        functools.partial(_mla_kv_kernel, heads=heads, bk=bq),
        out_shape=(jax.ShapeDtypeStruct((s, qk_width), BF16),
                   jax.ShapeDtypeStruct((heads, s // bq, vt_rows, bq), BF16)),
        grid=(s // bm,),
        in_specs=[pl.BlockSpec((bm, kv_rank), lambda i: (i, 0)),
                  pl.BlockSpec((bm, HEAD_DIM), lambda i: (i, kv_rank // HEAD_DIM)),
                  pl.BlockSpec((1, kv_rank), lambda i: (0, 0)),
                  pl.BlockSpec(w_ukv_p.shape, lambda i: (0, 0)), tab, tab],
        out_specs=(pl.BlockSpec((bm, qk_width), lambda i: (i, 0)),
                   pl.BlockSpec((heads, bm // bq, vt_rows, bq), lambda i: (0, i, 0, 0))),
        compiler_params=_params(("parallel",), vmem_kv),
        name="mla_kv_up",
    )(lat, lat, kv_norm_g.reshape(1, kv_rank), w_ukv_p, cos_t, sin_t)

    rows = 2 * bq
    assert s % rows == 0
    vmem_a = 2 * (_nbytes((MLA_QK_PAD, rows), BF16) + _nbytes((s, MLA_QK_PAD), BF16) + _nbytes((s, HEAD_DIM), BF16)
                  + _nbytes((rows, HEAD_DIM), BF16)) + 3 * _nbytes((HEAD_DIM, rows), F32) \
        + 12 * _nbytes((bq, bq), F32)
    return pl.pallas_call(
        functools.partial(_mla_flash_kernel, bk=bq),
        out_shape=jax.ShapeDtypeStruct((s, v_width), BF16),
        grid=(heads, s // rows),
        in_specs=[pl.BlockSpec((MLA_QK_PAD, rows), lambda h, i: (h, i)),
                  pl.BlockSpec((s, MLA_QK_PAD), lambda h, i: (0, h)),
                  pl.BlockSpec((None, s // bq, vt_rows, bq), lambda h, i: (h, 0, 0, 0))],
        out_specs=pl.BlockSpec((rows, HEAD_DIM), lambda h, i: (i, h)),
        scratch_shapes=[pltpu.VMEM((2, 1, bq), F32), pltpu.VMEM((2, vt_rows, bq), F32),
                        pltpu.VMEM((2, bq, bq), F32), pltpu.VMEM((2, 1, bq), F32)],
        compiler_params=_params(("parallel", "arbitrary"), vmem_a),
        name="mla_flash_attention",
    )(qt, k, vt)


def _pad_cols(w, width):
    return jnp.pad(w, ((0, 0), (0, width - w.shape[1])))


def _prep_layer(l, d, w_in, w_uq, w_ukv, w_gate, w_up, w_down, heads):
    width = d // 4
    q_rank = w_uq.shape[1]
    kv_rank = w_ukv.shape[1]
    n_main = 7 * width
    w = w_in[l]
    w_main = w[:, :n_main].astype(BF16)
    cq = w[:, n_main:n_main + q_rank]
    ckv = w[:, n_main + q_rank:n_main + q_rank + kv_rank]
    kr = w[:, n_main + q_rank + kv_rank:]
    w_lat = jnp.concatenate([ckv, _pad_cols(kr, q_rank - kv_rank), cq], axis=1).astype(BF16)
    uq = w_uq[l].reshape(q_rank, heads, HEAD_DIM + ROPE_DIM)
    uq = jnp.pad(uq, ((0, 0), (0, 0), (0, MLA_QK_PAD - HEAD_DIM - ROPE_DIM))).reshape(q_rank, heads * MLA_QK_PAD)
    ukv = w_ukv[l].reshape(kv_rank, heads, 2, HEAD_DIM).transpose(0, 2, 1, 3).reshape(kv_rank, 2 * heads * HEAD_DIM)
    return dict(
        w_main=w_main, w_lat=w_lat, w_uq=uq.astype(BF16), w_ukv=ukv.astype(BF16),
        w_gate=w_gate[l].astype(BF16), w_up=w_up[l].astype(BF16), w_down=w_down[l].astype(BF16),
    )


def kernel(x, w_in, w_pool, pool_scale, t5_bias, conv_w, q_norm_g, w_uq, kv_norm_g, w_ukv, w_out, ln1_g, ln1_b,
           w_gate, w_up, w_down, ln2_g, ln2_b):
    batch, s, d = x.shape
    depth = w_in.shape[0]
    width = d // 4
    heads = (d - 3 * width) // HEAD_DIM
    q_rank = w_uq.shape[1]
    kv_rank = w_ukv.shape[1]
    f = w_gate.shape[-1]
    alpha = (2 * depth) ** 0.25
    assert batch == 1 and width % (len(POOL_WINDOWS) * HEAD_DIM) == 0 and f % 256 == 0

    bm = _pick(s, (1024, 512, 256, 128))
    cos_t, sin_t = _rope_tables(s)
    xf = x.reshape(s, d)
    xb = xf.astype(BF16)
    for l in range(depth):
        p = _prep_layer(l, d, w_in, w_uq, w_ukv, w_gate, w_up, w_down, heads)
        proj = _matmul(xb, p["w_main"], BF16, bm, _pick(7 * width, (1024, 896, 512, 256, 128)), name="in_proj_main")
        lat = _matmul(xb, p["w_lat"], F32, bm, _pick(2 * q_rank, (768, 512, 384, 256, 128)), name="in_proj_latent")
        y_a, y_c = _local_mixers(proj, w_pool[l].astype(BF16), pool_scale[l], conv_w[l], width,
                                 _pick(s, (512, 256, 128)))
        y_b = _dilated_attention(proj, t5_bias, width)
        y_d = _mla(lat, q_rank, kv_rank, q_norm_g[l], p["w_uq"], kv_norm_g[l], p["w_ukv"], cos_t, sin_t, heads)
        y = _matmul([y_a, y_b, y_c, y_d], w_out[l].astype(BF16), F32, bm, _pick(d, (1024, 512, 256, 128)),
                    name="out_proj")
        xf, xb = _add_ln(y, xf, ln1_g[l], ln1_b[l], alpha, _pick(s, (256, 128)))
        hid = _gate_up(xb, p["w_gate"], p["w_up"], _pick(s, (2048, 1024, 512, 256, 128)), 256)
        y = _matmul(hid, p["w_down"], F32, _pick(s, (512, 256, 128)), _pick(d, (512, 256, 128)), name="ffn_down")
        xf, xb = _add_ln(y, xf, ln2_g[l], ln2_b[l], alpha, _pick(s, (256, 128)))
    return xf.reshape(batch, s, d)
```

```python
import functools
import math

import jax
import jax.numpy as jnp
from jax import lax
from jax.experimental import pallas as pl
from jax.experimental.pallas import tpu as pltpu

BLOCK = 128
HEAD_DIM = 128
ROPE_DIM = 64
MLA_QK_PAD = 256
ONES_ROWS = 16
POOL_WINDOWS = (2, 4, 8, 16)
DIL_PATTERNS = ((128, 1), (512, 4), (2048, 16))
N_BUCKETS = 32
T5_MAX_DIST = 2048
CONV_TAPS = 3
ROPE_THETA = 10000.0
LN_EPS = 1e-5
RMS_EPS = 1e-6
HALO = 16
NEG = -1e30

V7X_VMEM_BUDGET = 60 * 1024 * 1024
VMEM_SLACK = 8 * 1024 * 1024

F32 = jnp.float32
BF16 = jnp.bfloat16


def _params(sem, vmem_bytes):
    return pltpu.CompilerParams(dimension_semantics=sem,
                                vmem_limit_bytes=int(min(vmem_bytes + VMEM_SLACK, V7X_VMEM_BUDGET)))


def _nbytes(shape, dtype):
    return math.prod(shape) * jnp.dtype(dtype).itemsize


def _pick(n, prefs):
    for p in prefs:
        if n % p == 0:
            return p
    return n


def _mm_kernel(*refs, n_a, nk, alpha):
    a_refs, w_ref, rest = refs[:n_a], refs[n_a], refs[n_a + 1:]
    res_ref, o_ref = rest if alpha is not None else (None, rest[0])
    a = a_refs[0][...] if n_a == 1 else jnp.concatenate([r[...] for r in a_refs], axis=1)
    part = jnp.dot(a, w_ref[...], preferred_element_type=F32)
    if alpha is not None:
        part = part + alpha * res_ref[...]
    if nk == 1:
        o_ref[...] = part.astype(o_ref.dtype)
    else:
        k = pl.program_id(2)

        @pl.when(k == 0)
        def _():
            o_ref[...] = part

        @pl.when(k > 0)
        def _():
            o_ref[...] += part


def _matmul(a, w, layer, out_dtype, bm, bn, n=None, bk=None, residual=None, alpha=None, name="matmul"):
    parts = list(a) if isinstance(a, (list, tuple)) else [a]
    m = parts[0].shape[0]
    kdim = w.shape[1]
    n = w.shape[2] if n is None else n
    bk = kdim if bk is None else bk
    nk = kdim // bk
    assert m % bm == 0 and n % bn == 0 and kdim % bk == 0 and sum(p.shape[1] for p in parts) == kdim
    assert nk == 1 or (out_dtype == F32 and len(parts) == 1 and residual is None)
    if len(parts) == 1:
        specs = [pl.BlockSpec((bm, bk), lambda i, j, k: (i, k))]
    else:
        specs = [pl.BlockSpec((bm, p.shape[1]), lambda i, j, k: (i, 0)) for p in parts]
    specs.append(pl.BlockSpec((None, bk, bn), lambda i, j, k: (layer, k, j)))
    args = parts + [w]
    if residual is not None:
        specs.append(pl.BlockSpec((bm, bn), lambda i, j, k: (i, j)))
        args.append(residual)
    vmem = 2 * (_nbytes((bm, bk), parts[0].dtype) + _nbytes((bk, bn), w.dtype) + _nbytes((bm, bn), out_dtype)) \
        + 2 * _nbytes((bm, bn), F32) + (len(parts) > 1) * _nbytes((bm, bk), parts[0].dtype) \
        + (residual is not None) * 2 * _nbytes((bm, bn), F32)
    return pl.pallas_call(
        functools.partial(_mm_kernel, n_a=len(parts), nk=nk, alpha=alpha if residual is not None else None),
        out_shape=jax.ShapeDtypeStruct((m, n), out_dtype),
        grid=(m // bm, n // bn, nk),
        in_specs=specs,
        out_specs=pl.BlockSpec((bm, bn), lambda i, j, k: (i, j)),
        compiler_params=_params(("parallel", "parallel", "arbitrary"), vmem),
        name=name,
    )(*args)


def _gate_up_kernel(a_ref, wg_ref, wu_ref, o_ref):
    a = a_ref[...]
    g = jnp.dot(a, wg_ref[...], preferred_element_type=F32)
    u = jnp.dot(a, wu_ref[...], preferred_element_type=F32)
    o_ref[...] = (g * (1.0 / (1.0 + jnp.exp(-g))) * u).astype(o_ref.dtype)


def _gate_up(a, wg, wu, layer, bm, bn):
    m, kdim = a.shape
    n = wg.shape[2]
    w_spec = pl.BlockSpec((None, kdim, bn), lambda i, j: (layer, 0, j))
    vmem = 2 * (_nbytes((bm, kdim), a.dtype) + 2 * _nbytes((kdim, bn), wg.dtype) + _nbytes((bm, bn), BF16)) \
        + 4 * _nbytes((bm, bn), F32)
    return pl.pallas_call(
        _gate_up_kernel,
        out_shape=jax.ShapeDtypeStruct((m, n), BF16),
        grid=(m // bm, n // bn),
        in_specs=[pl.BlockSpec((bm, kdim), lambda i, j: (i, 0)), w_spec, w_spec],
        out_specs=pl.BlockSpec((bm, bn), lambda i, j: (i, j)),
        compiler_params=_params(("parallel", "parallel"), vmem),
        name="ffn_gate_up",
    )(a, wg, wu)


def _ln_kernel(z_ref, g_ref, b_ref, of_ref, ob_ref):
    z = z_ref[...]
    mu = jnp.mean(z, axis=-1, keepdims=True)
    zc = z - mu
    var = jnp.mean(zc * zc, axis=-1, keepdims=True)
    out = zc * lax.rsqrt(var + LN_EPS) * g_ref[...] + b_ref[...]
    of_ref[...] = out
    ob_ref[...] = out.astype(BF16)


def _layernorm(z, g, b, br):
    s, d = z.shape
    row = pl.BlockSpec((br, d), lambda i: (i, 0))
    vec = pl.BlockSpec((1, d), lambda i: (0, 0))
    vmem = 2 * (2 * _nbytes((br, d), F32) + _nbytes((br, d), BF16)) + 4 * _nbytes((br, d), F32)
    return pl.pallas_call(
        _ln_kernel,
        out_shape=(jax.ShapeDtypeStruct((s, d), F32), jax.ShapeDtypeStruct((s, d), BF16)),
        grid=(s // br,),
        in_specs=[row, vec, vec],
        out_specs=(row, row),
        compiler_params=_params(("parallel",), vmem),
        name="layernorm",
    )(z, g.reshape(1, d), b.reshape(1, d))


def _local_kernel(u_ref, uh_ref, hc_ref, hch_ref, bg_ref, cg_ref, cgh_ref, wp_ref, ps_ref, cw_ref,
                  ya_ref, yc_ref, us_ref, zs_ref, *, bs, group):
    i = pl.program_id(0)
    has_hist = i > 0
    us_ref[0:HALO, :] = jnp.where(has_hist, uh_ref[...].astype(F32), 0.0)
    us_ref[HALO:, :] = u_ref[...].astype(F32)
    zs_ref[0:HALO, :] = jnp.where(has_hist, cgh_ref[...].astype(F32) * hch_ref[...].astype(F32), 0.0)
    zs_ref[HALO:, :] = cg_ref[...].astype(F32) * hc_ref[...].astype(F32)

    pos = i * bs + lax.broadcasted_iota(jnp.int32, (bs, group), 0)
    for g, w in enumerate(POOL_WINDOWS):
        cols = slice(g * group, (g + 1) * group)
        cur = us_ref[HALO:HALO + bs, cols]
        acc = cur
        for j in range(1, w):
            acc = acc + us_ref[HALO - j:HALO - j + bs, cols]
        count = jnp.minimum(pos + 1, w).astype(F32)
        diff = acc / count - cur
        y = jnp.dot(diff.astype(BF16), wp_ref[g], preferred_element_type=F32)
        ya_ref[:, cols] = (y * ps_ref[:, cols]).astype(ya_ref.dtype)

    conv = cw_ref[0:1, :] * zs_ref[HALO - 2:HALO - 2 + bs, :]
    for tap in range(1, CONV_TAPS):
        off = HALO - (CONV_TAPS - 1) + tap
        conv = conv + cw_ref[tap:tap + 1, :] * zs_ref[off:off + bs, :]
    yc_ref[...] = (bg_ref[...].astype(F32) * conv).astype(yc_ref.dtype)


def _local_mixers(proj, w_pool, pool_scale, conv_w, width, bs):
    s = proj.shape[0]
    group = width // len(POOL_WINDOWS)
    hb = bs // HALO

    def blk(c):
        return pl.BlockSpec((bs, width), lambda i, c=c: (i, c))

    def halo(c):
        return pl.BlockSpec((HALO, width), lambda i, c=c: (jnp.maximum(i * hb - 1, 0), c))

    out = pl.BlockSpec((bs, width), lambda i: (i, 0))
    vmem = 2 * (6 * _nbytes((bs, width), BF16) + _nbytes(w_pool.shape, BF16)) \
        + 2 * _nbytes((bs + HALO, width), F32) + 6 * _nbytes((bs, width), F32)
    return pl.pallas_call(
        functools.partial(_local_kernel, bs=bs, group=group),
        out_shape=(jax.ShapeDtypeStruct((s, width), BF16), jax.ShapeDtypeStruct((s, width), BF16)),
        grid=(s // bs,),
        in_specs=[blk(0), halo(0), blk(4), halo(4), blk(5), blk(6), halo(6),
                  pl.BlockSpec(w_pool.shape, lambda i: (0, 0, 0)),
                  pl.BlockSpec((1, width), lambda i: (0, 0)),
                  pl.BlockSpec((CONV_TAPS, width), lambda i: (0, 0))],
        out_specs=(out, out),
        scratch_shapes=[pltpu.VMEM((bs + HALO, width), F32), pltpu.VMEM((bs + HALO, width), F32)],
        compiler_params=_params(("arbitrary",), vmem),
        name="pool_conv_mixers",
    )(proj, proj, proj, proj, proj, proj, proj, w_pool, pool_scale.reshape(1, width), conv_w)


def _t5_bucket(dist):
    exact = N_BUCKETS // 2
    d = jnp.maximum(dist, 1).astype(F32)
    large = exact + (jnp.log(d / exact) / math.log(T5_MAX_DIST / exact) * (N_BUCKETS - exact)).astype(jnp.int32)
    return jnp.where(dist < exact, dist, jnp.minimum(large, N_BUCKETS - 1))


def _band_bias(t5_bias, window, dilation):
    steps = window // dilation
    qi = jnp.arange(BLOCK)[:, None]
    ki = jnp.arange(2 * BLOCK)[None, :]
    dist = qi + BLOCK - ki
    bucket = _t5_bucket(jnp.clip(dist, 0) * dilation)
    hit = bucket[..., None, None] == jnp.arange(N_BUCKETS)[:, None]
    bias = jnp.sum(jnp.where(hit, t5_bias.astype(F32)[None, None], 0.0), axis=2)
    bias = jnp.where(((dist >= 0) & (dist <= steps))[..., None], bias, NEG)
    return jnp.moveaxis(bias, -1, 0)


def _dilated_kernel(q_ref, kc_ref, kp_ref, vc_ref, vp_ref, bias_ref, o_ref,
                    q_scr, k_scr, v_scr, part_o, part_l, *, tile, branches):
    t = pl.program_id(1)
    q_scr[...] = q_ref[...].astype(F32)
    k_scr[0:tile, :] = kp_ref[...].astype(F32)
    k_scr[tile:, :] = kc_ref[...].astype(F32)
    v_scr[0:tile, :] = vp_ref[...].astype(F32)
    v_scr[tile:, :] = vc_ref[...].astype(F32)
    scale = HEAD_DIM ** -0.5
    before_start = (t == 0) & (lax.broadcasted_iota(jnp.int32, (BLOCK, 2 * BLOCK), 1) < BLOCK)

    def rows_of(start, size, d):
        return pl.ds(start, size) if d == 1 else pl.ds(start, size, stride=d)

    for idx, d in enumerate(branches):
        last = idx == len(branches) - 1
        for r in range(d):
            for b in range(tile // (d * BLOCK)):
                q_rows = rows_of(r + d * BLOCK * b, BLOCK, d)
                k_rows = rows_of(tile + r + d * BLOCK * (b - 1), 2 * BLOCK, d)
                s = lax.dot_general(q_scr[q_rows, :].astype(BF16), k_scr[k_rows, :].astype(BF16),
                                    (((1,), (1,)), ((), ())), preferred_element_type=F32)
                s = s * scale + bias_ref[idx]
                if b == 0:
                    s = jnp.where(before_start, NEG, s)
                m = jnp.max(s, axis=-1, keepdims=True)
                p = jnp.exp(s - m)
                den = jnp.sum(p, axis=-1, keepdims=True)
                out = jnp.dot(p.astype(BF16), v_scr[k_rows, :].astype(BF16), preferred_element_type=F32) / den
                lse = m + jnp.log(den)
                if not last:
                    part_o[idx, q_rows, :] = out
                    part_l[idx, q_rows, :] = jnp.broadcast_to(lse, (BLOCK, HEAD_DIM))
                else:
                    lses = [part_l[e, q_rows, :] for e in range(idx)] + [lse]
                    outs = [part_o[e, q_rows, :] for e in range(idx)] + [out]
                    top = functools.reduce(jnp.maximum, lses)
                    wts = [jnp.exp(x - top) for x in lses]
                    num = functools.reduce(lambda x, y: x + y, [w * o for w, o in zip(wts, outs)])
                    o_ref[q_rows, :] = (num / functools.reduce(lambda x, y: x + y, wts)).astype(o_ref.dtype)


def _dilated_attention(proj, t5_bias, width):
    s = proj.shape[0]
    heads = width // HEAD_DIM
    order = sorted(DIL_PATTERNS, key=lambda p: -p[1])
    branches = tuple(d for _, d in order)
    tile = max(branches) * BLOCK
    assert branches[-1] == 1 and s % tile == 0 and all(w // d == BLOCK and tile % (d * BLOCK) == 0 for w, d in order)
    bias = jnp.stack([_band_bias(t5_bias, w, d) for w, d in order], axis=1)

    def cur(c):
        return pl.BlockSpec((tile, HEAD_DIM), lambda h, t, c=c: (t, c * heads + h))

    def prev(c):
        return pl.BlockSpec((tile, HEAD_DIM), lambda h, t, c=c: (jnp.maximum(t - 1, 0), c * heads + h))

    nb = len(branches)
    vmem = 2 * (6 * _nbytes((tile, HEAD_DIM), BF16) + _nbytes(bias.shape[1:], F32)) \
        + (5 + 2 * (nb - 1)) * _nbytes((tile, HEAD_DIM), F32) + 64 * _nbytes((BLOCK, 2 * BLOCK), F32)
    return pl.pallas_call(
        functools.partial(_dilated_kernel, tile=tile, branches=branches),
        out_shape=jax.ShapeDtypeStruct((s, width), BF16),
        grid=(heads, s // tile),
        in_specs=[cur(1), cur(2), prev(2), cur(3), prev(3),
                  pl.BlockSpec((None,) + bias.shape[1:], lambda h, t: (h, 0, 0, 0))],
        out_specs=pl.BlockSpec((tile, HEAD_DIM), lambda h, t: (t, h)),
        scratch_shapes=[pltpu.VMEM((tile, HEAD_DIM), F32), pltpu.VMEM((2 * tile, HEAD_DIM), F32),
                        pltpu.VMEM((2 * tile, HEAD_DIM), F32),
                        pltpu.VMEM((nb - 1, tile, HEAD_DIM), F32), pltpu.VMEM((nb - 1, tile, HEAD_DIM), F32)],
        compiler_params=_params(("parallel", "arbitrary"), vmem),
        name="dilated_attention",
    )(proj, proj, proj, proj, proj, bias)


def _rms(x, g):
    return x * lax.rsqrt(jnp.mean(x * x, axis=-1, keepdims=True) + RMS_EPS) * g


def _rope_chunk(chunk, cos_ref, sin_ref):
    half = ROPE_DIM // 2
    swapped = pltpu.roll(chunk, HEAD_DIM - half, 1) + pltpu.roll(chunk, half, 1)
    return chunk * cos_ref[...] + swapped * sin_ref[...]


def _mla_q_kernel(cq_ref, g_ref, w_ref, cos_ref, sin_ref, qt_ref, *, heads, scale):
    qn = _rms(cq_ref[...], g_ref[...]).astype(BF16)
    q = jnp.dot(qn, w_ref[...], preferred_element_type=F32)
    for h in range(heads):
        nope = q[:, h * MLA_QK_PAD:h * MLA_QK_PAD + HEAD_DIM]
        rope = _rope_chunk(q[:, h * MLA_QK_PAD + HEAD_DIM:(h + 1) * MLA_QK_PAD], cos_ref, sin_ref)
        qt_ref[h * MLA_QK_PAD:h * MLA_QK_PAD + HEAD_DIM, :] = (nope * scale).T.astype(qt_ref.dtype)
        qt_ref[h * MLA_QK_PAD + HEAD_DIM:(h + 1) * MLA_QK_PAD, :] = (rope * scale).T.astype(qt_ref.dtype)


def _mla_kv_kernel(ckv_ref, kr_ref, g_ref, w_ref, cos_ref, sin_ref, k_ref, vt_ref, *, heads, bk):
    kvn = _rms(ckv_ref[...], g_ref[...]).astype(BF16)
    kv = jnp.dot(kvn, w_ref[...], preferred_element_type=F32)
    k_rope = _rope_chunk(kr_ref[...], cos_ref, sin_ref).astype(k_ref.dtype)
    for h in range(heads):
        k_ref[:, h * MLA_QK_PAD:h * MLA_QK_PAD + HEAD_DIM] = \
            kv[:, h * HEAD_DIM:(h + 1) * HEAD_DIM].astype(k_ref.dtype)
        k_ref[:, h * MLA_QK_PAD + HEAD_DIM:(h + 1) * MLA_QK_PAD] = k_rope
        for c in range(vt_ref.shape[1]):
            v = kv[c * bk:(c + 1) * bk, (heads + h) * HEAD_DIM:(heads + h + 1) * HEAD_DIM]
            vt_ref[h, c, 0:HEAD_DIM, :] = v.T.astype(vt_ref.dtype)
            vt_ref[h, c, HEAD_DIM:, :] = jnp.ones((ONES_ROWS, bk), vt_ref.dtype)


def _mla_flash_kernel(qt_ref, k_ref, vt_ref, o_ref, m_scr, acc_scr, s_scr, smax_scr, *, bk):
    i = pl.program_id(1)
    m_scr[...] = jnp.full(m_scr.shape, NEG, F32)
    acc_scr[...] = jnp.zeros(acc_scr.shape, F32)
    qts = [qt_ref[:, c * bk:(c + 1) * bk] for c in range(2)]

    def keys(j):
        return k_ref[pl.ds(pl.multiple_of(j * bk, bk), bk), :]

    def logits(c, k_blk):
        s = jnp.dot(k_blk, qts[c], preferred_element_type=F32)
        return s, jnp.max(s, axis=0, keepdims=True)

    def consume(c, s, s_max, vt_blk, diagonal):
        if diagonal:
            key = lax.broadcasted_iota(jnp.int32, (bk, bk), 0)
            qry = lax.broadcasted_iota(jnp.int32, (bk, bk), 1)
            s = jnp.where(key <= qry, s, NEG)
            s_max = jnp.max(s, axis=0, keepdims=True)
        m_prev = m_scr[c]
        m_new = jnp.maximum(m_prev, s_max)
        alpha = jnp.exp2(m_prev - m_new)
        p = jnp.exp2(s - m_new).astype(BF16)
        acc_scr[c] = alpha * acc_scr[c] + jnp.dot(vt_blk, p, preferred_element_type=F32)
        m_scr[c] = m_new

    k_first = keys(0)
    for c in range(2):
        s_scr[c], smax_scr[c] = logits(c, k_first)

    def body(j, carry):
        k_next = keys(j + 1)
        vt_blk = vt_ref[j]
        for c in range(2):
            s, s_max = s_scr[c], smax_scr[c]
            s_scr[c], smax_scr[c] = logits(c, k_next)
            consume(c, s, s_max, vt_blk, False)
        return carry

    first_diag = 2 * i
    lax.fori_loop(0, first_diag, body, 0)
    s_last, _ = logits(1, keys(first_diag + 1))
    consume(0, s_scr[0], None, vt_ref[first_diag], True)
    consume(1, s_scr[1], smax_scr[1], vt_ref[first_diag], False)
    consume(1, s_last, None, vt_ref[first_diag + 1], True)
    for c in range(2):
        acc = acc_scr[c]
        o_ref[c * bk:(c + 1) * bk, :] = (acc[0:HEAD_DIM] / acc[HEAD_DIM:HEAD_DIM + 1]).T.astype(o_ref.dtype)


def _rope_tables(s):
    half = ROPE_DIM // 2
    pos = jnp.arange(s, dtype=F32)
    inv = ROPE_THETA ** (-jnp.arange(0, ROPE_DIM, 2, dtype=F32) / ROPE_DIM)
    ang = pos[:, None] * inv[None, :]
    cos, sin = jnp.cos(ang), jnp.sin(ang)
    zeros = jnp.zeros((s, HEAD_DIM - 2 * half), F32)
    return jnp.concatenate([cos, cos, zeros], axis=1), jnp.concatenate([-sin, sin, zeros], axis=1)


def _mla(lat, q_rank, kv_rank, q_norm_g, w_uq_p, kv_norm_g, w_ukv_p, cos_t, sin_t, heads):
    s = lat.shape[0]
    bm = _pick(s, (1024, 512, 256, 128))
    bq = _pick(s, (512, 256, 128))
    assert (q_rank - kv_rank) % HEAD_DIM == 0 and kv_rank % HEAD_DIM == 0 and bm % bq == 0
    tab = pl.BlockSpec((bm, HEAD_DIM), lambda i: (i, 0))
    qk_width = heads * MLA_QK_PAD
    v_width = heads * HEAD_DIM
    vt_rows = HEAD_DIM + ONES_ROWS
    scale = (HEAD_DIM + ROPE_DIM) ** -0.5 * math.log2(math.e)

    vmem_q = 2 * (_nbytes((bm, q_rank), F32) + _nbytes(w_uq_p.shape, BF16) + 2 * _nbytes((bm, HEAD_DIM), F32)
                  + _nbytes((bm, qk_width), BF16)) + 3 * _nbytes((bm, qk_width), F32)
    qt = pl.pallas_call(
        functools.partial(_mla_q_kernel, heads=heads, scale=scale),
        out_shape=jax.ShapeDtypeStruct((qk_width, s), BF16),
        grid=(s // bm,),
        in_specs=[pl.BlockSpec((bm, q_rank), lambda i: (i, 1)),
                  pl.BlockSpec((1, q_rank), lambda i: (0, 0)),
                  pl.BlockSpec(w_uq_p.shape, lambda i: (0, 0)), tab, tab],
        out_specs=pl.BlockSpec((qk_width, bm), lambda i: (0, i)),
        compiler_params=_params(("parallel",), vmem_q),
        name="mla_q_up",
    )(lat, q_norm_g.reshape(1, q_rank), w_uq_p, cos_t, sin_t)

    vmem_kv = 2 * (_nbytes((bm, kv_rank), F32) + _nbytes(w_ukv_p.shape, BF16) + 3 * _nbytes((bm, HEAD_DIM), F32)
                   + _nbytes((bm, qk_width), BF16) + _nbytes((bm, v_width), BF16)) \
        + 3 * _nbytes((bm, 2 * v_width), F32)
    k, vt = pl.pallas_call(
        functools.partial(_mla_kv_kernel, heads=heads, bk=bq),
        out_shape=(jax.ShapeDtypeStruct((s, qk_width), BF16),
                   jax.ShapeDtypeStruct((heads, s // bq, vt_rows, bq), BF16)),
        grid=(s // bm,),
        in_specs=[pl.BlockSpec((bm, kv_rank), lambda i: (i, 0)),
                  pl.BlockSpec((bm, HEAD_DIM), lambda i: (i, kv_rank // HEAD_DIM)),
                  pl.BlockSpec((1, kv_rank), lambda i: (0, 0)),
                  pl.BlockSpec(w_ukv_p.shape, lambda i: (0, 0)), tab, tab],
        out_specs=(pl.BlockSpec((bm, qk_width), lambda i: (i, 0)),
                   pl.BlockSpec((heads, bm // bq, vt_rows, bq), lambda i: (0, i, 0, 0))),
        compiler_params=_params(("parallel",), vmem_kv),
        name="mla_kv_up",
    )(lat, lat, kv_norm_g.reshape(1, kv_rank), w_ukv_p, cos_t, sin_t)

    rows = 2 * bq
    assert s % rows == 0
    vmem_a = 2 * (_nbytes((MLA_QK_PAD, rows), BF16) + _nbytes((s, MLA_QK_PAD), BF16) + _nbytes((s, HEAD_DIM), BF16)
                  + _nbytes((rows, HEAD_DIM), BF16)) + 3 * _nbytes((HEAD_DIM, rows), F32) \
        + 12 * _nbytes((bq, bq), F32)
    return pl.pallas_call(
        functools.partial(_mla_flash_kernel, bk=bq),
        out_shape=jax.ShapeDtypeStruct((s, v_width), BF16),
        grid=(heads, s // rows),
        in_specs=[pl.BlockSpec((MLA_QK_PAD, rows), lambda h, i: (h, i)),
                  pl.BlockSpec((s, MLA_QK_PAD), lambda h, i: (0, h)),
                  pl.BlockSpec((None, s // bq, vt_rows, bq), lambda h, i: (h, 0, 0, 0))],
        out_specs=pl.BlockSpec((rows, HEAD_DIM), lambda h, i: (i, h)),
        scratch_shapes=[pltpu.VMEM((2, 1, bq), F32), pltpu.VMEM((2, vt_rows, bq), F32),
                        pltpu.VMEM((2, bq, bq), F32), pltpu.VMEM((2, 1, bq), F32)],
        compiler_params=_params(("parallel", "arbitrary"), vmem_a),
        name="mla_flash_attention",
    )(qt, k, vt)


def _pad_cols(w, width):
    return jnp.pad(w, ((0, 0), (0, width - w.shape[1])))


def _prep_layer(l, d, w_in, w_uq, w_ukv, heads):
    width = d // 4
    q_rank = w_uq.shape[1]
    kv_rank = w_ukv.shape[1]
    n_main = 7 * width
    w = w_in[l]
    cq = w[:, n_main:n_main + q_rank]
    ckv = w[:, n_main + q_rank:n_main + q_rank + kv_rank]
    kr = w[:, n_main + q_rank + kv_rank:]
    w_lat = jnp.concatenate([ckv, _pad_cols(kr, q_rank - kv_rank), cq], axis=1).astype(BF16)
    uq = w_uq[l].reshape(q_rank, heads, HEAD_DIM + ROPE_DIM)
    uq = jnp.pad(uq, ((0, 0), (0, 0), (0, MLA_QK_PAD - HEAD_DIM - ROPE_DIM))).reshape(q_rank, heads * MLA_QK_PAD)
    ukv = w_ukv[l].reshape(kv_rank, heads, 2, HEAD_DIM).transpose(0, 2, 1, 3).reshape(kv_rank, 2 * heads * HEAD_DIM)
    return dict(w_lat=w_lat[None], w_uq=uq.astype(BF16), w_ukv=ukv.astype(BF16))


def kernel(x, w_in, w_pool, pool_scale, t5_bias, conv_w, q_norm_g, w_uq, kv_norm_g, w_ukv, w_out, ln1_g, ln1_b,
           w_gate, w_up, w_down, ln2_g, ln2_b):
    batch, s, d = x.shape
    depth = w_in.shape[0]
    width = d // 4
    heads = (d - 3 * width) // HEAD_DIM
    q_rank = w_uq.shape[1]
    kv_rank = w_ukv.shape[1]
    f = w_gate.shape[-1]
    alpha = (2 * depth) ** 0.25
    assert batch == 1 and width % (len(POOL_WINDOWS) * HEAD_DIM) == 0 and f % 256 == 0

    bm = _pick(s, (1024, 512, 256, 128))
    cos_t, sin_t = _rope_tables(s)
    w_in_b, w_out_b = w_in.astype(BF16), w_out.astype(BF16)
    w_gate_b, w_up_b, w_down_b = w_gate.astype(BF16), w_up.astype(BF16), w_down.astype(BF16)
    xf = x.reshape(s, d)
    xb = xf.astype(BF16)
    for l in range(depth):
        p = _prep_layer(l, d, w_in, w_uq, w_ukv, heads)
        proj = _matmul(xb, w_in_b, l, BF16, bm, _pick(7 * width, (1024, 896, 512, 256, 128)), n=7 * width,
                       name="in_proj_main")
        lat = _matmul(xb, p["w_lat"], 0, F32, bm, _pick(2 * q_rank, (768, 512, 384, 256, 128)), name="in_proj_latent")
        y_a, y_c = _local_mixers(proj, w_pool[l].astype(BF16), pool_scale[l], conv_w[l], width,
                                 _pick(s, (512, 256, 128)))
        y_b = _dilated_attention(proj, t5_bias, width)
        y_d = _mla(lat, q_rank, kv_rank, q_norm_g[l], p["w_uq"], kv_norm_g[l], p["w_ukv"], cos_t, sin_t, heads)
        z = _matmul([y_a, y_b, y_c, y_d], w_out_b, l, F32, bm, _pick(d, (512, 256, 128)), residual=xf, alpha=alpha,
                    name="out_proj")
        xf, xb = _layernorm(z, ln1_g[l], ln1_b[l], _pick(s, (256, 128)))
        hid = _gate_up(xb, w_gate_b, w_up_b, l, _pick(s, (2048, 1024, 512, 256, 128)), 256)
        z = _matmul(hid, w_down_b, l, F32, _pick(s, (512, 256, 128)), _pick(d, (512, 256, 128)), residual=xf,
                    alpha=alpha, name="ffn_down")
        xf, xb = _layernorm(z, ln2_g[l], ln2_b[l], _pick(s, (256, 128)))
    return xf.reshape(batch, s, d)
```

```python
import functools
import math

import jax
import jax.numpy as jnp
from jax import lax
from jax.experimental import pallas as pl
from jax.experimental.pallas import tpu as pltpu

BLOCK = 128
HEAD_DIM = 128
ROPE_DIM = 64
MLA_QK_PAD = 256
ONES_ROWS = 16
POOL_WINDOWS = (2, 4, 8, 16)
DIL_PATTERNS = ((128, 1), (512, 4), (2048, 16))
N_BUCKETS = 32
T5_MAX_DIST = 2048
CONV_TAPS = 3
ROPE_THETA = 10000.0
LN_EPS = 1e-5
RMS_EPS = 1e-6
HALO = 16
NEG = -1e30

V7X_VMEM_BUDGET = 60 * 1024 * 1024
VMEM_SLACK = 8 * 1024 * 1024

F32 = jnp.float32
BF16 = jnp.bfloat16


def _params(sem, vmem_bytes):
    return pltpu.CompilerParams(dimension_semantics=sem,
                                vmem_limit_bytes=int(min(vmem_bytes + VMEM_SLACK, V7X_VMEM_BUDGET)))


def _nbytes(shape, dtype):
    return math.prod(shape) * jnp.dtype(dtype).itemsize


def _pick(n, prefs):
    for p in prefs:
        if n % p == 0:
            return p
    return n


def _mm_kernel(*refs, n_a, nk, alpha):
    a_refs, w_ref, rest = refs[:n_a], refs[n_a], refs[n_a + 1:]
    res_ref, o_ref = rest if alpha is not None else (None, rest[0])
    part, row = None, 0
    for a_ref in a_refs:
        term = jnp.dot(a_ref[...], w_ref[row:row + a_ref.shape[1], :], preferred_element_type=F32)
        part = term if part is None else part + term
        row += a_ref.shape[1]
    if alpha is not None:
        part = part + alpha * res_ref[...]
    if nk == 1:
        o_ref[...] = part.astype(o_ref.dtype)
    else:
        k = pl.program_id(2)

        @pl.when(k == 0)
        def _():
            o_ref[...] = part

        @pl.when(k > 0)
        def _():
            o_ref[...] += part


def _matmul(a, w, layer, out_dtype, bm, bn, n=None, bk=None, residual=None, alpha=None, name="matmul"):
    parts = list(a) if isinstance(a, (list, tuple)) else [a]
    m = parts[0].shape[0]
    kdim = w.shape[1]
    n = w.shape[2] if n is None else n
    bk = kdim if bk is None else bk
    nk = kdim // bk
    assert m % bm == 0 and n % bn == 0 and kdim % bk == 0 and sum(p.shape[1] for p in parts) == kdim
    assert nk == 1 or (out_dtype == F32 and len(parts) == 1 and residual is None)
    if len(parts) == 1:
        specs = [pl.BlockSpec((bm, bk), lambda i, j, k: (i, k))]
    else:
        specs = [pl.BlockSpec((bm, p.shape[1]), lambda i, j, k: (i, 0)) for p in parts]
    specs.append(pl.BlockSpec((None, bk, bn), lambda i, j, k: (layer, k, j)))
    args = parts + [w]
    if residual is not None:
        specs.append(pl.BlockSpec((bm, bn), lambda i, j, k: (i, j)))
        args.append(residual)
    vmem = 2 * (_nbytes((bm, bk), parts[0].dtype) + _nbytes((bk, bn), w.dtype) + _nbytes((bm, bn), out_dtype)) \
        + 2 * _nbytes((bm, bn), F32) \
        + (residual is not None) * 2 * _nbytes((bm, bn), F32)
    return pl.pallas_call(
        functools.partial(_mm_kernel, n_a=len(parts), nk=nk, alpha=alpha if residual is not None else None),
        out_shape=jax.ShapeDtypeStruct((m, n), out_dtype),
        grid=(m // bm, n // bn, nk),
        in_specs=specs,
        out_specs=pl.BlockSpec((bm, bn), lambda i, j, k: (i, j)),
        compiler_params=_params(("parallel", "parallel", "arbitrary"), vmem),
        name=name,
    )(*args)


def _gate_up_kernel(a_ref, wg_ref, wu_ref, o_ref):
    a = a_ref[...]
    g = jnp.dot(a, wg_ref[...], preferred_element_type=F32)
    u = jnp.dot(a, wu_ref[...], preferred_element_type=F32)
    o_ref[...] = (g * (1.0 / (1.0 + jnp.exp(-g))) * u).astype(o_ref.dtype)


def _gate_up(a, wg, wu, layer, bm, bn):
    m, kdim = a.shape
    n = wg.shape[2]
    w_spec = pl.BlockSpec((None, kdim, bn), lambda i, j: (layer, 0, j))
    vmem = 2 * (_nbytes((bm, kdim), a.dtype) + 2 * _nbytes((kdim, bn), wg.dtype) + _nbytes((bm, bn), BF16)) \
        + 4 * _nbytes((bm, bn), F32)
    return pl.pallas_call(
        _gate_up_kernel,
        out_shape=jax.ShapeDtypeStruct((m, n), BF16),
        grid=(m // bm, n // bn),
        in_specs=[pl.BlockSpec((bm, kdim), lambda i, j: (i, 0)), w_spec, w_spec],
        out_specs=pl.BlockSpec((bm, bn), lambda i, j: (i, j)),
        compiler_params=_params(("parallel", "parallel"), vmem),
        name="ffn_gate_up",
    )(a, wg, wu)


def _ln_kernel(z_ref, g_ref, b_ref, of_ref, ob_ref):
    z = z_ref[...]
    mu = jnp.mean(z, axis=-1, keepdims=True)
    zc = z - mu
    var = jnp.mean(zc * zc, axis=-1, keepdims=True)
    out = zc * lax.rsqrt(var + LN_EPS) * g_ref[...] + b_ref[...]
    of_ref[...] = out
    ob_ref[...] = out.astype(BF16)


def _layernorm(z, g, b, br):
    s, d = z.shape
    row = pl.BlockSpec((br, d), lambda i: (i, 0))
    vec = pl.BlockSpec((1, d), lambda i: (0, 0))
    vmem = 2 * (2 * _nbytes((br, d), F32) + _nbytes((br, d), BF16)) + 4 * _nbytes((br, d), F32)
    return pl.pallas_call(
        _ln_kernel,
        out_shape=(jax.ShapeDtypeStruct((s, d), F32), jax.ShapeDtypeStruct((s, d), BF16)),
        grid=(s // br,),
        in_specs=[row, vec, vec],
        out_specs=(row, row),
        compiler_params=_params(("parallel",), vmem),
        name="layernorm",
    )(z, g.reshape(1, d), b.reshape(1, d))


def _local_kernel(u_ref, uh_ref, hc_ref, hch_ref, bg_ref, cg_ref, cgh_ref, wp_ref, ps_ref, cw_ref,
                  ya_ref, yc_ref, us_ref, zs_ref, *, bs, group):
    i = pl.program_id(0)
    has_hist = i > 0
    us_ref[0:HALO, :] = jnp.where(has_hist, uh_ref[...].astype(F32), 0.0)
    us_ref[HALO:, :] = u_ref[...].astype(F32)
    zs_ref[0:HALO, :] = jnp.where(has_hist, cgh_ref[...].astype(F32) * hch_ref[...].astype(F32), 0.0)
    zs_ref[HALO:, :] = cg_ref[...].astype(F32) * hc_ref[...].astype(F32)

    pos = i * bs + lax.broadcasted_iota(jnp.int32, (bs, group), 0)
    for g, w in enumerate(POOL_WINDOWS):
        cols = slice(g * group, (g + 1) * group)
        cur = us_ref[HALO:HALO + bs, cols]
        acc = cur
        for j in range(1, w):
            acc = acc + us_ref[HALO - j:HALO - j + bs, cols]
        count = jnp.minimum(pos + 1, w).astype(F32)
        diff = acc / count - cur
        y = jnp.dot(diff.astype(BF16), wp_ref[g], preferred_element_type=F32)
        ya_ref[:, cols] = (y * ps_ref[:, cols]).astype(ya_ref.dtype)

    conv = cw_ref[0:1, :] * zs_ref[HALO - 2:HALO - 2 + bs, :]
    for tap in range(1, CONV_TAPS):
        off = HALO - (CONV_TAPS - 1) + tap
        conv = conv + cw_ref[tap:tap + 1, :] * zs_ref[off:off + bs, :]
    yc_ref[...] = (bg_ref[...].astype(F32) * conv).astype(yc_ref.dtype)


def _local_mixers(proj, w_pool, pool_scale, conv_w, width, bs):
    s = proj.shape[0]
    group = width // len(POOL_WINDOWS)
    hb = bs // HALO

    def blk(c):
        return pl.BlockSpec((bs, width), lambda i, c=c: (i, c))

    def halo(c):
        return pl.BlockSpec((HALO, width), lambda i, c=c: (jnp.maximum(i * hb - 1, 0), c))

    out = pl.BlockSpec((bs, width), lambda i: (i, 0))
    vmem = 2 * (6 * _nbytes((bs, width), BF16) + _nbytes(w_pool.shape, BF16)) \
        + 2 * _nbytes((bs + HALO, width), F32) + 6 * _nbytes((bs, width), F32)
    return pl.pallas_call(
        functools.partial(_local_kernel, bs=bs, group=group),
        out_shape=(jax.ShapeDtypeStruct((s, width), BF16), jax.ShapeDtypeStruct((s, width), BF16)),
        grid=(s // bs,),
        in_specs=[blk(0), halo(0), blk(4), halo(4), blk(5), blk(6), halo(6),
                  pl.BlockSpec(w_pool.shape, lambda i: (0, 0, 0)),
                  pl.BlockSpec((1, width), lambda i: (0, 0)),
                  pl.BlockSpec((CONV_TAPS, width), lambda i: (0, 0))],
        out_specs=(out, out),
        scratch_shapes=[pltpu.VMEM((bs + HALO, width), F32), pltpu.VMEM((bs + HALO, width), F32)],
        compiler_params=_params(("arbitrary",), vmem),
        name="pool_conv_mixers",
    )(proj, proj, proj, proj, proj, proj, proj, w_pool, pool_scale.reshape(1, width), conv_w)


def _t5_bucket(dist):
    exact = N_BUCKETS // 2
    d = jnp.maximum(dist, 1).astype(F32)
    large = exact + (jnp.log(d / exact) / math.log(T5_MAX_DIST / exact) * (N_BUCKETS - exact)).astype(jnp.int32)
    return jnp.where(dist < exact, dist, jnp.minimum(large, N_BUCKETS - 1))


def _band_bias(t5_bias, window, dilation):
    steps = window // dilation
    qi = jnp.arange(BLOCK)[:, None]
    ki = jnp.arange(2 * BLOCK)[None, :]
    dist = qi + BLOCK - ki
    bucket = _t5_bucket(jnp.clip(dist, 0) * dilation)
    hit = bucket[..., None, None] == jnp.arange(N_BUCKETS)[:, None]
    bias = jnp.sum(jnp.where(hit, t5_bias.astype(F32)[None, None], 0.0), axis=2)
    bias = jnp.where(((dist >= 0) & (dist <= steps))[..., None], bias, NEG)
    return jnp.moveaxis(bias, -1, 0)


def _dilated_kernel(q_ref, kc_ref, kp_ref, vc_ref, vp_ref, bias_ref, o_ref,
                    q_scr, k_scr, v_scr, part_o, part_l, *, tile, branches):
    t = pl.program_id(1)
    q_scr[...] = q_ref[...].astype(F32)
    k_scr[0:tile, :] = kp_ref[...].astype(F32)
    k_scr[tile:, :] = kc_ref[...].astype(F32)
    v_scr[0:tile, :] = vp_ref[...].astype(F32)
    v_scr[tile:, :] = vc_ref[...].astype(F32)
    scale = HEAD_DIM ** -0.5
    before_start = (t == 0) & (lax.broadcasted_iota(jnp.int32, (BLOCK, 2 * BLOCK), 1) < BLOCK)

    def rows_of(start, size, d):
        return pl.ds(start, size) if d == 1 else pl.ds(start, size, stride=d)

    for idx, d in enumerate(branches):
        last = idx == len(branches) - 1
        for r in range(d):
            for b in range(tile // (d * BLOCK)):
                q_rows = rows_of(r + d * BLOCK * b, BLOCK, d)
                k_rows = rows_of(tile + r + d * BLOCK * (b - 1), 2 * BLOCK, d)
                s = lax.dot_general(q_scr[q_rows, :].astype(BF16), k_scr[k_rows, :].astype(BF16),
                                    (((1,), (1,)), ((), ())), preferred_element_type=F32)
                s = s * scale + bias_ref[idx]
                if b == 0:
                    s = jnp.where(before_start, NEG, s)
                m = jnp.max(s, axis=-1, keepdims=True)
                p = jnp.exp(s - m)
                den = jnp.sum(p, axis=-1, keepdims=True)
                out = jnp.dot(p.astype(BF16), v_scr[k_rows, :].astype(BF16), preferred_element_type=F32) / den
                lse = m + jnp.log(den)
                if not last:
                    part_o[idx, q_rows, :] = out
                    part_l[idx, q_rows, :] = jnp.broadcast_to(lse, (BLOCK, HEAD_DIM))
                else:
                    lses = [part_l[e, q_rows, :] for e in range(idx)] + [lse]
                    outs = [part_o[e, q_rows, :] for e in range(idx)] + [out]
                    top = functools.reduce(jnp.maximum, lses)
                    wts = [jnp.exp(x - top) for x in lses]
                    num = functools.reduce(lambda x, y: x + y, [w * o for w, o in zip(wts, outs)])
                    o_ref[q_rows, :] = (num / functools.reduce(lambda x, y: x + y, wts)).astype(o_ref.dtype)


def _dilated_attention(proj, t5_bias, width):
    s = proj.shape[0]
    heads = width // HEAD_DIM
    order = sorted(DIL_PATTERNS, key=lambda p: -p[1])
    branches = tuple(d for _, d in order)
    tile = max(branches) * BLOCK
    assert branches[-1] == 1 and s % tile == 0 and all(w // d == BLOCK and tile % (d * BLOCK) == 0 for w, d in order)
    bias = jnp.stack([_band_bias(t5_bias, w, d) for w, d in order], axis=1)

    def cur(c):
        return pl.BlockSpec((tile, HEAD_DIM), lambda h, t, c=c: (t, c * heads + h))

    def prev(c):
        return pl.BlockSpec((tile, HEAD_DIM), lambda h, t, c=c: (jnp.maximum(t - 1, 0), c * heads + h))

    nb = len(branches)
    vmem = 2 * (6 * _nbytes((tile, HEAD_DIM), BF16) + _nbytes(bias.shape[1:], F32)) \
        + (5 + 2 * (nb - 1)) * _nbytes((tile, HEAD_DIM), F32) + 64 * _nbytes((BLOCK, 2 * BLOCK), F32)
    return pl.pallas_call(
        functools.partial(_dilated_kernel, tile=tile, branches=branches),
        out_shape=jax.ShapeDtypeStruct((s, width), BF16),
        grid=(heads, s // tile),
        in_specs=[cur(1), cur(2), prev(2), cur(3), prev(3),
                  pl.BlockSpec((None,) + bias.shape[1:], lambda h, t: (h, 0, 0, 0))],
        out_specs=pl.BlockSpec((tile, HEAD_DIM), lambda h, t: (t, h)),
        scratch_shapes=[pltpu.VMEM((tile, HEAD_DIM), F32), pltpu.VMEM((2 * tile, HEAD_DIM), F32),
                        pltpu.VMEM((2 * tile, HEAD_DIM), F32),
                        pltpu.VMEM((nb - 1, tile, HEAD_DIM), F32), pltpu.VMEM((nb - 1, tile, HEAD_DIM), F32)],
        compiler_params=_params(("parallel", "arbitrary"), vmem),
        name="dilated_attention",
    )(proj, proj, proj, proj, proj, bias)


def _rms(x, g):
    return x * lax.rsqrt(jnp.mean(x * x, axis=-1, keepdims=True) + RMS_EPS) * g


def _rope_chunk(chunk, cos_ref, sin_ref):
    half = ROPE_DIM // 2
    swapped = pltpu.roll(chunk, HEAD_DIM - half, 1) + pltpu.roll(chunk, half, 1)
    return chunk * cos_ref[...] + swapped * sin_ref[...]


def _mla_q_kernel(cq_ref, g_ref, w_ref, cos_ref, sin_ref, qt_ref, *, heads, scale):
    qn = _rms(cq_ref[...], g_ref[...]).astype(BF16)
    q = jnp.dot(qn, w_ref[...], preferred_element_type=F32)
    for h in range(heads):
        nope = q[:, h * MLA_QK_PAD:h * MLA_QK_PAD + HEAD_DIM]
        rope = _rope_chunk(q[:, h * MLA_QK_PAD + HEAD_DIM:(h + 1) * MLA_QK_PAD], cos_ref, sin_ref)
        qt_ref[h * MLA_QK_PAD:h * MLA_QK_PAD + HEAD_DIM, :] = (nope * scale).T.astype(qt_ref.dtype)
        qt_ref[h * MLA_QK_PAD + HEAD_DIM:(h + 1) * MLA_QK_PAD, :] = (rope * scale).T.astype(qt_ref.dtype)


def _mla_kv_kernel(ckv_ref, kr_ref, g_ref, w_ref, cos_ref, sin_ref, k_ref, vt_ref, *, heads, bk):
    kvn = _rms(ckv_ref[...], g_ref[...]).astype(BF16)
    kv = jnp.dot(kvn, w_ref[...], preferred_element_type=F32)
    k_rope = _rope_chunk(kr_ref[...], cos_ref, sin_ref).astype(k_ref.dtype)
    for h in range(heads):
        k_ref[:, h * MLA_QK_PAD:h * MLA_QK_PAD + HEAD_DIM] = \
            kv[:, h * HEAD_DIM:(h + 1) * HEAD_DIM].astype(k_ref.dtype)
        k_ref[:, h * MLA_QK_PAD + HEAD_DIM:(h + 1) * MLA_QK_PAD] = k_rope
        for c in range(vt_ref.shape[1]):
            v = kv[c * bk:(c + 1) * bk, (heads + h) * HEAD_DIM:(heads + h + 1) * HEAD_DIM]
            vt_ref[h, c, 0:HEAD_DIM, :] = v.T.astype(vt_ref.dtype)
            vt_ref[h, c, HEAD_DIM:, :] = jnp.ones((ONES_ROWS, bk), vt_ref.dtype)


def _mla_flash_kernel(qt_ref, k_ref, vt_ref, o_ref, m_scr, acc_scr, s_scr, smax_scr, *, bq, bk):
    chains = bk // bq
    i = pl.program_id(1)
    m_scr[...] = jnp.full(m_scr.shape, NEG, F32)
    acc_scr[...] = jnp.zeros(acc_scr.shape, F32)
    qts = [qt_ref[:, c * bq:(c + 1) * bq] for c in range(chains)]

    def keys(j):
        return k_ref[pl.ds(pl.multiple_of(j * bk, bk), bk), :]

    def logits(c, k_blk):
        s = jnp.dot(k_blk, qts[c], preferred_element_type=F32)
        return s, jnp.max(s, axis=0, keepdims=True)

    def consume(c, s, s_max, vt_blk, diagonal):
        if diagonal:
            key = lax.broadcasted_iota(jnp.int32, (bk, bq), 0)
            qry = lax.broadcasted_iota(jnp.int32, (bk, bq), 1) + c * bq
            s = jnp.where(key <= qry, s, NEG)
            s_max = jnp.max(s, axis=0, keepdims=True)
        m_prev = m_scr[c]
        m_new = jnp.maximum(m_prev, s_max)
        alpha = jnp.exp2(m_prev - m_new)
        p = jnp.exp2(s - m_new).astype(BF16)
        acc_scr[c] = alpha * acc_scr[c] + jnp.dot(vt_blk, p, preferred_element_type=F32)
        m_scr[c] = m_new

    k_first = keys(0)
    for c in range(chains):
        s_scr[c], smax_scr[c] = logits(c, k_first)

    def body(j, carry):
        k_next = keys(j + 1)
        vt_blk = vt_ref[j]
        for c in range(chains):
            s, s_max = s_scr[c], smax_scr[c]
            s_scr[c], smax_scr[c] = logits(c, k_next)
            consume(c, s, s_max, vt_blk, False)
        return carry

    lax.fori_loop(0, i, body, 0)
    vt_blk = vt_ref[i]
    for c in range(chains):
        consume(c, s_scr[c], None, vt_blk, True)
        acc = acc_scr[c]
        o_ref[c * bq:(c + 1) * bq, :] = (acc[0:HEAD_DIM] / acc[HEAD_DIM:HEAD_DIM + 1]).T.astype(o_ref.dtype)


def _rope_tables(s):
    half = ROPE_DIM // 2
    pos = jnp.arange(s, dtype=F32)
    inv = ROPE_THETA ** (-jnp.arange(0, ROPE_DIM, 2, dtype=F32) / ROPE_DIM)
    ang = pos[:, None] * inv[None, :]
    cos, sin = jnp.cos(ang), jnp.sin(ang)
    zeros = jnp.zeros((s, HEAD_DIM - 2 * half), F32)
    return jnp.concatenate([cos, cos, zeros], axis=1), jnp.concatenate([-sin, sin, zeros], axis=1)


def _mla(lat, q_rank, kv_rank, q_norm_g, w_uq_p, kv_norm_g, w_ukv_p, cos_t, sin_t, heads):
    s = lat.shape[0]
    bm = _pick(s, (1024, 512, 256, 128))
    bq = _pick(s, (512, 256, 128))
    bk = 2 * bq
    assert (q_rank - kv_rank) % HEAD_DIM == 0 and kv_rank % HEAD_DIM == 0 and bm % bk == 0 and s % bk == 0
    tab = pl.BlockSpec((bm, HEAD_DIM), lambda i: (i, 0))
    qk_width = heads * MLA_QK_PAD
    v_width = heads * HEAD_DIM
    vt_rows = HEAD_DIM + ONES_ROWS
    scale = (HEAD_DIM + ROPE_DIM) ** -0.5 * math.log2(math.e)

    vmem_q = 2 * (_nbytes((bm, q_rank), F32) + _nbytes(w_uq_p.shape, BF16) + 2 * _nbytes((bm, HEAD_DIM), F32)
                  + _nbytes((bm, qk_width), BF16)) + 3 * _nbytes((bm, qk_width), F32)
    qt = pl.pallas_call(
        functools.partial(_mla_q_kernel, heads=heads, scale=scale),
        out_shape=jax.ShapeDtypeStruct((qk_width, s), BF16),
        grid=(s // bm,),
        in_specs=[pl.BlockSpec((bm, q_rank), lambda i: (i, 1)),
                  pl.BlockSpec((1, q_rank), lambda i: (0, 0)),
                  pl.BlockSpec(w_uq_p.shape, lambda i: (0, 0)), tab, tab],
        out_specs=pl.BlockSpec((qk_width, bm), lambda i: (0, i)),
        compiler_params=_params(("parallel",), vmem_q),
        name="mla_q_up",
    )(lat, q_norm_g.reshape(1, q_rank), w_uq_p, cos_t, sin_t)

    vmem_kv = 2 * (_nbytes((bm, kv_rank), F32) + _nbytes(w_ukv_p.shape, BF16) + 3 * _nbytes((bm, HEAD_DIM), F32)
                   + _nbytes((bm, qk_width), BF16) + _nbytes((bm, v_width), BF16)) \
        + 3 * _nbytes((bm, 2 * v_width), F32)
    k, vt = pl.pallas_call(
        functools.partial(_mla_kv_kernel, heads=heads, bk=bk),
        out_shape=(jax.ShapeDtypeStruct((s, qk_width), BF16),
                   jax.ShapeDtypeStruct((heads, s // bk, vt_rows, bk), BF16)),
        grid=(s // bm,),
        in_specs=[pl.BlockSpec((bm, kv_rank), lambda i: (i, 0)),
                  pl.BlockSpec((bm, HEAD_DIM), lambda i: (i, kv_rank // HEAD_DIM)),
                  pl.BlockSpec((1, kv_rank), lambda i: (0, 0)),
                  pl.BlockSpec(w_ukv_p.shape, lambda i: (0, 0)), tab, tab],
        out_specs=(pl.BlockSpec((bm, qk_width), lambda i: (i, 0)),
                   pl.BlockSpec((heads, bm // bk, vt_rows, bk), lambda i: (0, i, 0, 0))),
        compiler_params=_params(("parallel",), vmem_kv),
        name="mla_kv_up",
    )(lat, lat, kv_norm_g.reshape(1, kv_rank), w_ukv_p, cos_t, sin_t)

    chains = bk // bq
    vmem_a = 2 * (_nbytes((MLA_QK_PAD, bk), BF16) + _nbytes((s, MLA_QK_PAD), BF16) + _nbytes((s, vt_rows), BF16)
                  + _nbytes((bk, HEAD_DIM), BF16)) + chains * (_nbytes((vt_rows, bq), F32) + 5 * _nbytes((bk, bq), F32))
    return pl.pallas_call(
        functools.partial(_mla_flash_kernel, bq=bq, bk=bk),
        out_shape=jax.ShapeDtypeStruct((s, v_width), BF16),
        grid=(heads, s // bk),
        in_specs=[pl.BlockSpec((MLA_QK_PAD, bk), lambda h, i: (h, i)),
                  pl.BlockSpec((s, MLA_QK_PAD), lambda h, i: (0, h)),
                  pl.BlockSpec((None, s // bk, vt_rows, bk), lambda h, i: (h, 0, 0, 0))],
        out_specs=pl.BlockSpec((bk, HEAD_DIM), lambda h, i: (i, h)),
        scratch_shapes=[pltpu.VMEM((chains, 1, bq), F32), pltpu.VMEM((chains, vt_rows, bq), F32),
                        pltpu.VMEM((chains, bk, bq), F32), pltpu.VMEM((chains, 1, bq), F32)],
        compiler_params=_params(("parallel", "arbitrary"), vmem_a),
        name="mla_flash_attention",
    )(qt, k, vt)


def _prep_weights(d, w_in, w_uq, w_ukv, heads):
    depth = w_in.shape[0]
    q_rank = w_uq.shape[1]
    kv_rank = w_ukv.shape[1]
    n_main = 7 * (d // 4)
    w_main = w_in[:, :, :n_main].astype(BF16)
    cq = w_in[:, :, n_main:n_main + q_rank]
    ckv = w_in[:, :, n_main + q_rank:n_main + q_rank + kv_rank]
    kr = w_in[:, :, n_main + q_rank + kv_rank:]
    kr = jnp.pad(kr, ((0, 0), (0, 0), (0, q_rank - kv_rank - kr.shape[2])))
    w_lat = jnp.concatenate([ckv, kr, cq], axis=2).astype(BF16)
    uq = w_uq.reshape(depth, q_rank, heads, HEAD_DIM + ROPE_DIM)
    uq = jnp.pad(uq, ((0, 0), (0, 0), (0, 0), (0, MLA_QK_PAD - HEAD_DIM - ROPE_DIM)))
    uq = uq.reshape(depth, q_rank, heads * MLA_QK_PAD).astype(BF16)
    ukv = w_ukv.reshape(depth, kv_rank, heads, 2, HEAD_DIM).transpose(0, 1, 3, 2, 4)
    ukv = ukv.reshape(depth, kv_rank, 2 * heads * HEAD_DIM).astype(BF16)
    return w_main, w_lat, uq, ukv


def kernel(x, w_in, w_pool, pool_scale, t5_bias, conv_w, q_norm_g, w_uq, kv_norm_g, w_ukv, w_out, ln1_g, ln1_b,
           w_gate, w_up, w_down, ln2_g, ln2_b):
    batch, s, d = x.shape
    depth = w_in.shape[0]
    width = d // 4
    heads = (d - 3 * width) // HEAD_DIM
    q_rank = w_uq.shape[1]
    kv_rank = w_ukv.shape[1]
    f = w_gate.shape[-1]
    alpha = (2 * depth) ** 0.25
    assert batch == 1 and width % (len(POOL_WINDOWS) * HEAD_DIM) == 0 and f % 256 == 0

    bm = _pick(s, (1024, 512, 256, 128))
    cos_t, sin_t = _rope_tables(s)
    w_main_b, w_lat_b, w_uq_b, w_ukv_b = _prep_weights(d, w_in, w_uq, w_ukv, heads)
    w_out_b, w_gate_b, w_up_b, w_down_b = (w.astype(BF16) for w in (w_out, w_gate, w_up, w_down))
    xf = x.reshape(s, d)
    xb = xf.astype(BF16)
    for l in range(depth):
        proj = _matmul(xb, w_main_b, l, BF16, bm, _pick(7 * width, (1024, 896, 512, 256, 128)), name="in_proj_main")
        lat = _matmul(xb, w_lat_b, l, F32, bm, _pick(2 * q_rank, (768, 512, 384, 256, 128)), name="in_proj_latent")
        y_a, y_c = _local_mixers(proj, w_pool[l].astype(BF16), pool_scale[l], conv_w[l], width,
                                 _pick(s, (512, 256, 128)))
        y_b = _dilated_attention(proj, t5_bias, width)
        y_d = _mla(lat, q_rank, kv_rank, q_norm_g[l], w_uq_b[l], kv_norm_g[l], w_ukv_b[l], cos_t, sin_t, heads)
        z = _matmul([y_a, y_b, y_c, y_d], w_out_b, l, F32, bm, _pick(d, (1024, 512, 256, 128)), residual=xf, alpha=alpha,
                    name="out_proj")
        xf, xb = _layernorm(z, ln1_g[l], ln1_b[l], _pick(s, (256, 128)))
        hid = _gate_up(xb, w_gate_b, w_up_b, l, _pick(s, (2048, 1024, 512, 256, 128)), 256)
        z = _matmul(hid, w_down_b, l, F32, _pick(s, (512, 256, 128)), _pick(d, (512, 256, 128)), residual=xf,
                    alpha=alpha, name="ffn_down")
        xf, xb = _layernorm(z, ln2_g[l], ln2_b[l], _pick(s, (256, 128)))
    return xf.reshape(batch, s, d)
```

```python
import functools
import math

import jax
import jax.numpy as jnp
from jax import lax
from jax.experimental import pallas as pl
from jax.experimental.pallas import tpu as pltpu

BLOCK = 128
HEAD_DIM = 128
ROPE_DIM = 64
MLA_QK_PAD = 256
ONES_ROWS = 16
POOL_WINDOWS = (2, 4, 8, 16)
DIL_PATTERNS = ((128, 1), (512, 4), (2048, 16))
N_BUCKETS = 32
T5_MAX_DIST = 2048
CONV_TAPS = 3
ROPE_THETA = 10000.0
LN_EPS = 1e-5
RMS_EPS = 1e-6
HALO = 16
NEG = -1e30
LOG2_E = math.log2(math.e)

V7X_VMEM_BUDGET = 60 * 1024 * 1024
VMEM_SLACK = 8 * 1024 * 1024

F32 = jnp.float32
BF16 = jnp.bfloat16


def _params(sem, vmem_bytes):
    return pltpu.CompilerParams(dimension_semantics=sem,
                                vmem_limit_bytes=int(min(vmem_bytes + VMEM_SLACK, V7X_VMEM_BUDGET)))


def _nbytes(shape, dtype):
    return math.prod(shape) * jnp.dtype(dtype).itemsize


def _pick(n, prefs):
    for p in prefs:
        if n % p == 0:
            return p
    return n


def _mm_kernel(*refs, n_a, nk, alpha):
    a_refs, w_ref, rest = refs[:n_a], refs[n_a], refs[n_a + 1:]
    res_ref, o_ref = rest if alpha is not None else (None, rest[0])
    part, row = None, 0
    for a_ref in a_refs:
        term = jnp.dot(a_ref[...], w_ref[row:row + a_ref.shape[1], :], preferred_element_type=F32)
        part = term if part is None else part + term
        row += a_ref.shape[1]
    if alpha is not None:
        part = part + alpha * res_ref[...]
    if nk == 1:
        o_ref[...] = part.astype(o_ref.dtype)
    else:
        k = pl.program_id(2)

        @pl.when(k == 0)
        def _():
            o_ref[...] = part

        @pl.when(k > 0)
        def _():
            o_ref[...] += part


def _matmul(a, w, layer, out_dtype, bm, bn, n=None, bk=None, residual=None, alpha=None, name="matmul"):
    parts = list(a) if isinstance(a, (list, tuple)) else [a]
    m = parts[0].shape[0]
    kdim = w.shape[1]
    n = w.shape[2] if n is None else n
    bk = kdim if bk is None else bk
    nk = kdim // bk
    assert m % bm == 0 and n % bn == 0 and kdim % bk == 0 and sum(p.shape[1] for p in parts) == kdim
    assert nk == 1 or (out_dtype == F32 and len(parts) == 1 and residual is None)
    if len(parts) == 1:
        specs = [pl.BlockSpec((bm, bk), lambda i, j, k: (i, k))]
    else:
        specs = [pl.BlockSpec((bm, p.shape[1]), lambda i, j, k: (i, 0)) for p in parts]
    specs.append(pl.BlockSpec((None, bk, bn), lambda i, j, k: (layer, k, j)))
    args = parts + [w]
    if residual is not None:
        specs.append(pl.BlockSpec((bm, bn), lambda i, j, k: (i, j)))
        args.append(residual)
    vmem = 2 * (_nbytes((bm, bk), parts[0].dtype) + _nbytes((bk, bn), w.dtype) + _nbytes((bm, bn), out_dtype)) \
        + 2 * _nbytes((bm, bn), F32) \
        + (residual is not None) * 2 * _nbytes((bm, bn), F32)
    return pl.pallas_call(
        functools.partial(_mm_kernel, n_a=len(parts), nk=nk, alpha=alpha if residual is not None else None),
        out_shape=jax.ShapeDtypeStruct((m, n), out_dtype),
        grid=(m // bm, n // bn, nk),
        in_specs=specs,
        out_specs=pl.BlockSpec((bm, bn), lambda i, j, k: (i, j)),
        compiler_params=_params(("parallel", "parallel", "arbitrary"), vmem),
        name=name,
    )(*args)


def _gate_up_kernel(a_ref, wg_ref, wu_ref, o_ref):
    a = a_ref[...]
    g = jnp.dot(a, wg_ref[...], preferred_element_type=F32)
    u = jnp.dot(a, wu_ref[...], preferred_element_type=F32)
    o_ref[...] = (g * (1.0 / (1.0 + jnp.exp(-g))) * u).astype(o_ref.dtype)


def _gate_up(a, wg, wu, layer, bm, bn):
    m, kdim = a.shape
    n = wg.shape[2]
    w_spec = pl.BlockSpec((None, kdim, bn), lambda i, j: (layer, 0, j))
    vmem = 2 * (_nbytes((bm, kdim), a.dtype) + 2 * _nbytes((kdim, bn), wg.dtype) + _nbytes((bm, bn), BF16)) \
        + 4 * _nbytes((bm, bn), F32)
    return pl.pallas_call(
        _gate_up_kernel,
        out_shape=jax.ShapeDtypeStruct((m, n), BF16),
        grid=(m // bm, n // bn),
        in_specs=[pl.BlockSpec((bm, kdim), lambda i, j: (i, 0)), w_spec, w_spec],
        out_specs=pl.BlockSpec((bm, bn), lambda i, j: (i, j)),
        compiler_params=_params(("parallel", "parallel"), vmem),
        name="ffn_gate_up",
    )(a, wg, wu)


def _ln_kernel(z_ref, g_ref, b_ref, of_ref, ob_ref):
    z = z_ref[...]
    mu = jnp.mean(z, axis=-1, keepdims=True)
    zc = z - mu
    var = jnp.mean(zc * zc, axis=-1, keepdims=True)
    out = zc * lax.rsqrt(var + LN_EPS) * g_ref[...] + b_ref[...]
    of_ref[...] = out
    ob_ref[...] = out.astype(BF16)


def _layernorm(z, g, b, br):
    s, d = z.shape
    row = pl.BlockSpec((br, d), lambda i: (i, 0))
    vec = pl.BlockSpec((1, d), lambda i: (0, 0))
    vmem = 2 * (2 * _nbytes((br, d), F32) + _nbytes((br, d), BF16)) + 4 * _nbytes((br, d), F32)
    return pl.pallas_call(
        _ln_kernel,
        out_shape=(jax.ShapeDtypeStruct((s, d), F32), jax.ShapeDtypeStruct((s, d), BF16)),
        grid=(s // br,),
        in_specs=[row, vec, vec],
        out_specs=(row, row),
        compiler_params=_params(("parallel",), vmem),
        name="layernorm",
    )(z, g.reshape(1, d), b.reshape(1, d))


def _local_kernel(u_ref, uh_ref, hc_ref, hch_ref, bg_ref, cg_ref, cgh_ref, wp_ref, ps_ref, cw_ref,
                  ya_ref, yc_ref, us_ref, zs_ref, *, bs, group):
    i = pl.program_id(0)
    has_hist = i > 0
    us_ref[0:HALO, :] = jnp.where(has_hist, uh_ref[...].astype(F32), 0.0)
    us_ref[HALO:, :] = u_ref[...].astype(F32)
    zs_ref[0:HALO, :] = jnp.where(has_hist, cgh_ref[...].astype(F32) * hch_ref[...].astype(F32), 0.0)
    zs_ref[HALO:, :] = cg_ref[...].astype(F32) * hc_ref[...].astype(F32)

    pos = i * bs + lax.broadcasted_iota(jnp.int32, (bs, group), 0)
    for g, w in enumerate(POOL_WINDOWS):
        cols = slice(g * group, (g + 1) * group)
        cur = us_ref[HALO:HALO + bs, cols]
        acc = cur
        for j in range(1, w):
            acc = acc + us_ref[HALO - j:HALO - j + bs, cols]
        count = jnp.minimum(pos + 1, w).astype(F32)
        diff = acc / count - cur
        y = jnp.dot(diff.astype(BF16), wp_ref[g], preferred_element_type=F32)
        ya_ref[:, cols] = (y * ps_ref[:, cols]).astype(ya_ref.dtype)

    conv = cw_ref[0:1, :] * zs_ref[HALO - 2:HALO - 2 + bs, :]
    for tap in range(1, CONV_TAPS):
        off = HALO - (CONV_TAPS - 1) + tap
        conv = conv + cw_ref[tap:tap + 1, :] * zs_ref[off:off + bs, :]
    yc_ref[...] = (bg_ref[...].astype(F32) * conv).astype(yc_ref.dtype)


def _local_mixers(proj, w_pool, pool_scale, conv_w, width, bs):
    s = proj.shape[0]
    group = width // len(POOL_WINDOWS)
    hb = bs // HALO

    def blk(c):
        return pl.BlockSpec((bs, width), lambda i, c=c: (i, c))

    def halo(c):
        return pl.BlockSpec((HALO, width), lambda i, c=c: (jnp.maximum(i * hb - 1, 0), c))

    out = pl.BlockSpec((bs, width), lambda i: (i, 0))
    vmem = 2 * (6 * _nbytes((bs, width), BF16) + _nbytes(w_pool.shape, BF16)) \
        + 2 * _nbytes((bs + HALO, width), F32) + 6 * _nbytes((bs, width), F32)
    return pl.pallas_call(
        functools.partial(_local_kernel, bs=bs, group=group),
        out_shape=(jax.ShapeDtypeStruct((s, width), BF16), jax.ShapeDtypeStruct((s, width), BF16)),
        grid=(s // bs,),
        in_specs=[blk(0), halo(0), blk(4), halo(4), blk(5), blk(6), halo(6),
                  pl.BlockSpec(w_pool.shape, lambda i: (0, 0, 0)),
                  pl.BlockSpec((1, width), lambda i: (0, 0)),
                  pl.BlockSpec((CONV_TAPS, width), lambda i: (0, 0))],
        out_specs=(out, out),
        scratch_shapes=[pltpu.VMEM((bs + HALO, width), F32), pltpu.VMEM((bs + HALO, width), F32)],
        compiler_params=_params(("arbitrary",), vmem),
        name="pool_conv_mixers",
    )(proj, proj, proj, proj, proj, proj, proj, w_pool, pool_scale.reshape(1, width), conv_w)


def _t5_bucket(dist):
    exact = N_BUCKETS // 2
    d = jnp.maximum(dist, 1).astype(F32)
    large = exact + (jnp.log(d / exact) / math.log(T5_MAX_DIST / exact) * (N_BUCKETS - exact)).astype(jnp.int32)
    return jnp.where(dist < exact, dist, jnp.minimum(large, N_BUCKETS - 1))


def _band_bias(t5_bias, window, dilation):
    steps = window // dilation
    qi = jnp.arange(BLOCK)[:, None]
    ki = jnp.arange(2 * BLOCK)[None, :]
    dist = qi + BLOCK - ki
    bucket = _t5_bucket(jnp.clip(dist, 0) * dilation)
    hit = bucket[..., None, None] == jnp.arange(N_BUCKETS)[:, None]
    bias = jnp.sum(jnp.where(hit, t5_bias.astype(F32)[None, None], 0.0), axis=2)
    bias = jnp.where(((dist >= 0) & (dist <= steps))[..., None], bias, NEG)
    return jnp.moveaxis(bias, -1, 0)


def _dilated_kernel(q_ref, kc_ref, kp_ref, vc_ref, vp_ref, bias_ref, o_ref,
                    q_scr, k_scr, v_scr, part_o, part_l, *, tile, branches):
    t = pl.program_id(1)
    q_scr[...] = q_ref[...].astype(F32) * (HEAD_DIM ** -0.5 * LOG2_E)
    k_scr[0:tile, :] = kp_ref[...].astype(F32)
    k_scr[tile:, :] = kc_ref[...].astype(F32)
    v_scr[0:tile, :] = vp_ref[...].astype(F32)
    v_scr[tile:, :] = vc_ref[...].astype(F32)
    before_start = (t == 0) & (lax.broadcasted_iota(jnp.int32, (BLOCK, 2 * BLOCK), 1) < BLOCK)

    def rows_of(start, size, d):
        return pl.ds(start, size) if d == 1 else pl.ds(start, size, stride=d)

    for idx, d in enumerate(branches):
        last = idx == len(branches) - 1
        for r in range(d):
            for b in range(tile // (d * BLOCK)):
                q_rows = rows_of(r + d * BLOCK * b, BLOCK, d)
                k_rows = rows_of(tile + r + d * BLOCK * (b - 1), 2 * BLOCK, d)
                s = lax.dot_general(q_scr[q_rows, :].astype(BF16), k_scr[k_rows, :].astype(BF16),
                                    (((1,), (1,)), ((), ())), preferred_element_type=F32)
                s = s + bias_ref[idx]
                if b == 0:
                    s = jnp.where(before_start, NEG, s)
                m = jnp.max(s, axis=-1, keepdims=True)
                p = jnp.exp2(s - m)
                den = jnp.sum(p, axis=-1, keepdims=True)
                out = jnp.dot(p.astype(BF16), v_scr[k_rows, :].astype(BF16), preferred_element_type=F32) / den
                lse = m + jnp.log2(den)
                if not last:
                    part_o[idx, q_rows, :] = out
                    part_l[idx, q_rows, :] = jnp.broadcast_to(lse, (BLOCK, HEAD_DIM))
                else:
                    lses = [part_l[e, q_rows, :] for e in range(idx)] + [lse]
                    outs = [part_o[e, q_rows, :] for e in range(idx)] + [out]
                    top = functools.reduce(jnp.maximum, lses)
                    wts = [jnp.exp2(x - top) for x in lses]
                    num = functools.reduce(lambda x, y: x + y, [w * o for w, o in zip(wts, outs)])
                    o_ref[q_rows, :] = (num / functools.reduce(lambda x, y: x + y, wts)).astype(o_ref.dtype)


def _dilated_attention(proj, t5_bias, width):
    s = proj.shape[0]
    heads = width // HEAD_DIM
    order = sorted(DIL_PATTERNS, key=lambda p: -p[1])
    branches = tuple(d for _, d in order)
    tile = max(branches) * BLOCK
    assert branches[-1] == 1 and s % tile == 0 and all(w // d == BLOCK and tile % (d * BLOCK) == 0 for w, d in order)
    bias = jnp.stack([_band_bias(t5_bias, w, d) for w, d in order], axis=1) * LOG2_E

    def cur(c):
        return pl.BlockSpec((tile, HEAD_DIM), lambda h, t, c=c: (t, c * heads + h))

    def prev(c):
        return pl.BlockSpec((tile, HEAD_DIM), lambda h, t, c=c: (jnp.maximum(t - 1, 0), c * heads + h))

    nb = len(branches)
    vmem = 2 * (6 * _nbytes((tile, HEAD_DIM), BF16) + _nbytes(bias.shape[1:], F32)) \
        + (5 + 2 * (nb - 1)) * _nbytes((tile, HEAD_DIM), F32) + 64 * _nbytes((BLOCK, 2 * BLOCK), F32)
    return pl.pallas_call(
        functools.partial(_dilated_kernel, tile=tile, branches=branches),
        out_shape=jax.ShapeDtypeStruct((s, width), BF16),
        grid=(heads, s // tile),
        in_specs=[cur(1), cur(2), prev(2), cur(3), prev(3),
                  pl.BlockSpec((None,) + bias.shape[1:], lambda h, t: (h, 0, 0, 0))],
        out_specs=pl.BlockSpec((tile, HEAD_DIM), lambda h, t: (t, h)),
        scratch_shapes=[pltpu.VMEM((tile, HEAD_DIM), F32), pltpu.VMEM((2 * tile, HEAD_DIM), F32),
                        pltpu.VMEM((2 * tile, HEAD_DIM), F32),
                        pltpu.VMEM((nb - 1, tile, HEAD_DIM), F32), pltpu.VMEM((nb - 1, tile, HEAD_DIM), F32)],
        compiler_params=_params(("parallel", "arbitrary"), vmem),
        name="dilated_attention",
    )(proj, proj, proj, proj, proj, bias)


def _rms(x, g):
    return x * lax.rsqrt(jnp.mean(x * x, axis=-1, keepdims=True) + RMS_EPS) * g


def _rope_chunk(chunk, cos_ref, sin_ref):
    half = ROPE_DIM // 2
    swapped = pltpu.roll(chunk, HEAD_DIM - half, 1) + pltpu.roll(chunk, half, 1)
    return chunk * cos_ref[...] + swapped * sin_ref[...]


def _mla_q_kernel(cq_ref, g_ref, w_ref, cos_ref, sin_ref, qt_ref, *, heads, scale):
    qn = _rms(cq_ref[...], g_ref[...]).astype(BF16)
    q = jnp.dot(qn, w_ref[...], preferred_element_type=F32)
    for h in range(heads):
        nope = q[:, h * MLA_QK_PAD:h * MLA_QK_PAD + HEAD_DIM]
        rope = _rope_chunk(q[:, h * MLA_QK_PAD + HEAD_DIM:(h + 1) * MLA_QK_PAD], cos_ref, sin_ref)
        qt_ref[h * MLA_QK_PAD:h * MLA_QK_PAD + HEAD_DIM, :] = (nope * scale).T.astype(qt_ref.dtype)
        qt_ref[h * MLA_QK_PAD + HEAD_DIM:(h + 1) * MLA_QK_PAD, :] = (rope * scale).T.astype(qt_ref.dtype)


def _mla_kv_kernel(ckv_ref, kr_ref, g_ref, w_ref, cos_ref, sin_ref, k_ref, vt_ref, *, heads, bk):
    kvn = _rms(ckv_ref[...], g_ref[...]).astype(BF16)
    kv = jnp.dot(kvn, w_ref[...], preferred_element_type=F32)
    k_rope = _rope_chunk(kr_ref[...], cos_ref, sin_ref).astype(k_ref.dtype)
    for h in range(heads):
        k_ref[:, h * MLA_QK_PAD:h * MLA_QK_PAD + HEAD_DIM] = \
            kv[:, h * HEAD_DIM:(h + 1) * HEAD_DIM].astype(k_ref.dtype)
        k_ref[:, h * MLA_QK_PAD + HEAD_DIM:(h + 1) * MLA_QK_PAD] = k_rope
        for c in range(vt_ref.shape[1]):
            v = kv[c * bk:(c + 1) * bk, (heads + h) * HEAD_DIM:(heads + h + 1) * HEAD_DIM]
            vt_ref[h, c, 0:HEAD_DIM, :] = v.T.astype(vt_ref.dtype)
            vt_ref[h, c, HEAD_DIM:, :] = jnp.ones((ONES_ROWS, bk), vt_ref.dtype)


def _mla_flash_kernel(qt_ref, k_ref, vt_ref, o_ref, m_scr, acc_scr, s_scr, smax_scr, *, bq, bk):
    chains = bk // bq
    i = pl.program_id(1)
    m_scr[...] = jnp.full(m_scr.shape, NEG, F32)
    acc_scr[...] = jnp.zeros(acc_scr.shape, F32)
    qts = [qt_ref[:, c * bq:(c + 1) * bq] for c in range(chains)]

    def keys(j):
        return k_ref[pl.ds(pl.multiple_of(j * bk, bk), bk), :]

    def logits(c, k_blk):
        s = jnp.dot(k_blk, qts[c], preferred_element_type=F32)
        return s, jnp.max(s, axis=0, keepdims=True)

    def consume(c, s, s_max, vt_blk, diagonal):
        if diagonal:
            key = lax.broadcasted_iota(jnp.int32, (bk, bq), 0)
            qry = lax.broadcasted_iota(jnp.int32, (bk, bq), 1) + c * bq
            s = jnp.where(key <= qry, s, NEG)
            s_max = jnp.max(s, axis=0, keepdims=True)
        m_prev = m_scr[c]
        m_new = jnp.maximum(m_prev, s_max)
        alpha = jnp.exp2(m_prev - m_new)
        p = jnp.exp2(s - m_new).astype(BF16)
        acc_scr[c] = alpha * acc_scr[c] + jnp.dot(vt_blk, p, preferred_element_type=F32)
        m_scr[c] = m_new

    k_first = keys(0)
    for c in range(chains):
        s_scr[c], smax_scr[c] = logits(c, k_first)

    def body(j, carry):
        k_next = keys(j + 1)
        vt_blk = vt_ref[j]
        for c in range(chains):
            s, s_max = s_scr[c], smax_scr[c]
            s_scr[c], smax_scr[c] = logits(c, k_next)
            consume(c, s, s_max, vt_blk, False)
        return carry

    lax.fori_loop(0, i, body, 0)
    vt_blk = vt_ref[i]
    for c in range(chains):
        consume(c, s_scr[c], None, vt_blk, True)
        acc = acc_scr[c]
        o_ref[c * bq:(c + 1) * bq, :] = (acc[0:HEAD_DIM] / acc[HEAD_DIM:HEAD_DIM + 1]).T.astype(o_ref.dtype)


def _rope_tables(s):
    half = ROPE_DIM // 2
    pos = jnp.arange(s, dtype=F32)
    inv = ROPE_THETA ** (-jnp.arange(0, ROPE_DIM, 2, dtype=F32) / ROPE_DIM)
    ang = pos[:, None] * inv[None, :]
    cos, sin = jnp.cos(ang), jnp.sin(ang)
    zeros = jnp.zeros((s, HEAD_DIM - 2 * half), F32)
    return jnp.concatenate([cos, cos, zeros], axis=1), jnp.concatenate([-sin, sin, zeros], axis=1)


def _mla(lat, q_rank, kv_rank, q_norm_g, w_uq_p, kv_norm_g, w_ukv_p, cos_t, sin_t, heads):
    s = lat.shape[0]
    bm = _pick(s, (1024, 512, 256, 128))
    bq = _pick(s, (512, 256, 128))
    bk = 2 * bq
    assert (q_rank - kv_rank) % HEAD_DIM == 0 and kv_rank % HEAD_DIM == 0 and bm % bk == 0 and s % bk == 0
    tab = pl.BlockSpec((bm, HEAD_DIM), lambda i: (i, 0))
    qk_width = heads * MLA_QK_PAD
    v_width = heads * HEAD_DIM
    vt_rows = HEAD_DIM + ONES_ROWS
    scale = (HEAD_DIM + ROPE_DIM) ** -0.5 * LOG2_E

    vmem_q = 2 * (_nbytes((bm, q_rank), F32) + _nbytes(w_uq_p.shape, BF16) + 2 * _nbytes((bm, HEAD_DIM), F32)
                  + _nbytes((bm, qk_width), BF16)) + 3 * _nbytes((bm, qk_width), F32)
    qt = pl.pallas_call(
        functools.partial(_mla_q_kernel, heads=heads, scale=scale),
        out_shape=jax.ShapeDtypeStruct((qk_width, s), BF16),
        grid=(s // bm,),
        in_specs=[pl.BlockSpec((bm, q_rank), lambda i: (i, 1)),
                  pl.BlockSpec((1, q_rank), lambda i: (0, 0)),
                  pl.BlockSpec(w_uq_p.shape, lambda i: (0, 0)), tab, tab],
        out_specs=pl.BlockSpec((qk_width, bm), lambda i: (0, i)),
        compiler_params=_params(("parallel",), vmem_q),
        name="mla_q_up",
    )(lat, q_norm_g.reshape(1, q_rank), w_uq_p, cos_t, sin_t)

    vmem_kv = 2 * (_nbytes((bm, kv_rank), F32) + _nbytes(w_ukv_p.shape, BF16) + 3 * _nbytes((bm, HEAD_DIM), F32)
                   + _nbytes((bm, qk_width), BF16) + _nbytes((bm, v_width), BF16)) \
        + 3 * _nbytes((bm, 2 * v_width), F32)
    k, vt = pl.pallas_call(
        functools.partial(_mla_kv_kernel, heads=heads, bk=bk),
        out_shape=(jax.ShapeDtypeStruct((s, qk_width), BF16),
                   jax.ShapeDtypeStruct((heads, s // bk, vt_rows, bk), BF16)),
        grid=(s // bm,),
        in_specs=[pl.BlockSpec((bm, kv_rank), lambda i: (i, 0)),
                  pl.BlockSpec((bm, HEAD_DIM), lambda i: (i, kv_rank // HEAD_DIM)),
                  pl.BlockSpec((1, kv_rank), lambda i: (0, 0)),
                  pl.BlockSpec(w_ukv_p.shape, lambda i: (0, 0)), tab, tab],
        out_specs=(pl.BlockSpec((bm, qk_width), lambda i: (i, 0)),
                   pl.BlockSpec((heads, bm // bk, vt_rows, bk), lambda i: (0, i, 0, 0))),
        compiler_params=_params(("parallel",), vmem_kv),
        name="mla_kv_up",
    )(lat, lat, kv_norm_g.reshape(1, kv_rank), w_ukv_p, cos_t, sin_t)

    chains = bk // bq
    vmem_a = 2 * (_nbytes((MLA_QK_PAD, bk), BF16) + _nbytes((s, MLA_QK_PAD), BF16) + _nbytes((s, vt_rows), BF16)
                  + _nbytes((bk, HEAD_DIM), BF16)) + chains * (_nbytes((vt_rows, bq), F32) + 5 * _nbytes((bk, bq), F32))
    return pl.pallas_call(
        functools.partial(_mla_flash_kernel, bq=bq, bk=bk),
        out_shape=jax.ShapeDtypeStruct((s, v_width), BF16),
        grid=(heads, s // bk),
        in_specs=[pl.BlockSpec((MLA_QK_PAD, bk), lambda h, i: (h, i)),
                  pl.BlockSpec((s, MLA_QK_PAD), lambda h, i: (0, h)),
                  pl.BlockSpec((None, s // bk, vt_rows, bk), lambda h, i: (h, 0, 0, 0))],
        out_specs=pl.BlockSpec((bk, HEAD_DIM), lambda h, i: (i, h)),
        scratch_shapes=[pltpu.VMEM((chains, 1, bq), F32), pltpu.VMEM((chains, vt_rows, bq), F32),
                        pltpu.VMEM((chains, bk, bq), F32), pltpu.VMEM((chains, 1, bq), F32)],
        compiler_params=_params(("parallel", "arbitrary"), vmem_a),
        name="mla_flash_attention",
    )(qt, k, vt)


def _prep_weights(d, w_in, w_uq, w_ukv, heads):
    depth = w_in.shape[0]
    q_rank = w_uq.shape[1]
    kv_rank = w_ukv.shape[1]
    n_main = 7 * (d // 4)
    w_main = w_in[:, :, :n_main].astype(BF16)
    cq = w_in[:, :, n_main:n_main + q_rank]
    ckv = w_in[:, :, n_main + q_rank:n_main + q_rank + kv_rank]
    kr = w_in[:, :, n_main + q_rank + kv_rank:]
    kr = jnp.pad(kr, ((0, 0), (0, 0), (0, q_rank - kv_rank - kr.shape[2])))
    w_lat = jnp.concatenate([ckv, kr, cq], axis=2).astype(BF16)
    uq = w_uq.reshape(depth, q_rank, heads, HEAD_DIM + ROPE_DIM)
    uq = jnp.pad(uq, ((0, 0), (0, 0), (0, 0), (0, MLA_QK_PAD - HEAD_DIM - ROPE_DIM)))
    uq = uq.reshape(depth, q_rank, heads * MLA_QK_PAD).astype(BF16)
    ukv = w_ukv.reshape(depth, kv_rank, heads, 2, HEAD_DIM).transpose(0, 1, 3, 2, 4)
    ukv = ukv.reshape(depth, kv_rank, 2 * heads * HEAD_DIM).astype(BF16)
    return w_main, w_lat, uq, ukv


def kernel(x, w_in, w_pool, pool_scale, t5_bias, conv_w, q_norm_g, w_uq, kv_norm_g, w_ukv, w_out, ln1_g, ln1_b,
           w_gate, w_up, w_down, ln2_g, ln2_b):
    batch, s, d = x.shape
    depth = w_in.shape[0]
    width = d // 4
    heads = (d - 3 * width) // HEAD_DIM
    q_rank = w_uq.shape[1]
    kv_rank = w_ukv.shape[1]
    f = w_gate.shape[-1]
    alpha = (2 * depth) ** 0.25
    assert batch == 1 and width % (len(POOL_WINDOWS) * HEAD_DIM) == 0 and f % 256 == 0

    bm = _pick(s, (1024, 512, 256, 128))
    cos_t, sin_t = _rope_tables(s)
    w_main_b, w_lat_b, w_uq_b, w_ukv_b = _prep_weights(d, w_in, w_uq, w_ukv, heads)
    w_out_b, w_gate_b, w_up_b, w_down_b = (w.astype(BF16) for w in (w_out, w_gate, w_up, w_down))
    xf = x.reshape(s, d)
    xb = xf.astype(BF16)
    for l in range(depth):
        proj = _matmul(xb, w_main_b, l, BF16, bm, _pick(7 * width, (1024, 896, 512, 256, 128)), name="in_proj_main")
        lat = _matmul(xb, w_lat_b, l, F32, bm, _pick(2 * q_rank, (768, 512, 384, 256, 128)), name="in_proj_latent")
        y_a, y_c = _local_mixers(proj, w_pool[l].astype(BF16), pool_scale[l], conv_w[l], width,
                                 _pick(s, (512, 256, 128)))
        y_b = _dilated_attention(proj, t5_bias, width)
        y_d = _mla(lat, q_rank, kv_rank, q_norm_g[l], w_uq_b[l], kv_norm_g[l], w_ukv_b[l], cos_t, sin_t, heads)
        z = _matmul([y_a, y_b, y_c, y_d], w_out_b, l, F32, bm, _pick(d, (1024, 512, 256, 128)), residual=xf, alpha=alpha,
                    name="out_proj")
        xf, xb = _layernorm(z, ln1_g[l], ln1_b[l], _pick(s, (256, 128)))
        hid = _gate_up(xb, w_gate_b, w_up_b, l, _pick(s, (2048, 1024, 512, 256, 128)), 256)
        z = _matmul(hid, w_down_b, l, F32, _pick(s, (512, 256, 128)), _pick(d, (512, 256, 128)), residual=xf,
                    alpha=alpha, name="ffn_down")
        xf, xb = _layernorm(z, ln2_g[l], ln2_b[l], _pick(s, (256, 128)))
    return xf.reshape(batch, s, d)
```

```python
import functools
import math

import jax
import jax.numpy as jnp
from jax import lax
from jax.experimental import pallas as pl
from jax.experimental.pallas import tpu as pltpu

BLOCK = 128
HEAD_DIM = 128
ROPE_DIM = 64
MLA_QK_PAD = 256
ONES_ROWS = 16
POOL_WINDOWS = (2, 4, 8, 16)
DIL_PATTERNS = ((128, 1), (512, 4), (2048, 16))
N_BUCKETS = 32
T5_MAX_DIST = 2048
CONV_TAPS = 3
ROPE_THETA = 10000.0
LN_EPS = 1e-5
RMS_EPS = 1e-6
HALO = 16
NEG = -1e30
LOG2_E = math.log2(math.e)

V7X_VMEM_BUDGET = 60 * 1024 * 1024
VMEM_SLACK = 8 * 1024 * 1024

F32 = jnp.float32
BF16 = jnp.bfloat16


def _params(sem, vmem_bytes):
    return pltpu.CompilerParams(dimension_semantics=sem,
                                vmem_limit_bytes=int(min(vmem_bytes + VMEM_SLACK, V7X_VMEM_BUDGET)))


def _nbytes(shape, dtype):
    return math.prod(shape) * jnp.dtype(dtype).itemsize


def _pick(n, prefs):
    for p in prefs:
        if n % p == 0:
            return p
    return n


def _mm_kernel(*refs, n_a, nk, alpha):
    a_refs, w_ref, rest = refs[:n_a], refs[n_a], refs[n_a + 1:]
    res_ref, o_ref = rest if alpha is not None else (None, rest[0])
    part, row = None, 0
    for a_ref in a_refs:
        term = jnp.dot(a_ref[...], w_ref[row:row + a_ref.shape[1], :], preferred_element_type=F32)
        part = term if part is None else part + term
        row += a_ref.shape[1]
    if alpha is not None:
        part = part + alpha * res_ref[...]
    if nk == 1:
        o_ref[...] = part.astype(o_ref.dtype)
    else:
        k = pl.program_id(2)

        @pl.when(k == 0)
        def _():
            o_ref[...] = part

        @pl.when(k > 0)
        def _():
            o_ref[...] += part


def _matmul(a, w, layer, out_dtype, bm, bn, n=None, bk=None, residual=None, alpha=None, name="matmul"):
    parts = list(a) if isinstance(a, (list, tuple)) else [a]
    m = parts[0].shape[0]
    kdim = w.shape[1]
    n = w.shape[2] if n is None else n
    bk = kdim if bk is None else bk
    nk = kdim // bk
    assert m % bm == 0 and n % bn == 0 and kdim % bk == 0 and sum(p.shape[1] for p in parts) == kdim
    assert nk == 1 or (out_dtype == F32 and len(parts) == 1 and residual is None)
    if len(parts) == 1:
        specs = [pl.BlockSpec((bm, bk), lambda i, j, k: (i, k))]
    else:
        specs = [pl.BlockSpec((bm, p.shape[1]), lambda i, j, k: (i, 0)) for p in parts]
    specs.append(pl.BlockSpec((None, bk, bn), lambda i, j, k: (layer, k, j)))
    args = parts + [w]
    if residual is not None:
        specs.append(pl.BlockSpec((bm, bn), lambda i, j, k: (i, j)))
        args.append(residual)
    vmem = 2 * (_nbytes((bm, bk), parts[0].dtype) + _nbytes((bk, bn), w.dtype) + _nbytes((bm, bn), out_dtype)) \
        + 2 * _nbytes((bm, bn), F32) \
        + (residual is not None) * 2 * _nbytes((bm, bn), F32)
    return pl.pallas_call(
        functools.partial(_mm_kernel, n_a=len(parts), nk=nk, alpha=alpha if residual is not None else None),
        out_shape=jax.ShapeDtypeStruct((m, n), out_dtype),
        grid=(m // bm, n // bn, nk),
        in_specs=specs,
        out_specs=pl.BlockSpec((bm, bn), lambda i, j, k: (i, j)),
        compiler_params=_params(("parallel", "parallel", "arbitrary"), vmem),
        name=name,
    )(*args)


def _gate_up_kernel(a_ref, wg_ref, wu_ref, o_ref):
    a = a_ref[...]
    g = jnp.dot(a, wg_ref[...], preferred_element_type=F32)
    u = jnp.dot(a, wu_ref[...], preferred_element_type=F32)
    o_ref[...] = (g * (1.0 / (1.0 + jnp.exp(-g))) * u).astype(o_ref.dtype)


def _gate_up(a, wg, wu, layer, bm, bn):
    m, kdim = a.shape
    n = wg.shape[2]
    w_spec = pl.BlockSpec((None, kdim, bn), lambda i, j: (layer, 0, j))
    vmem = 2 * (_nbytes((bm, kdim), a.dtype) + 2 * _nbytes((kdim, bn), wg.dtype) + _nbytes((bm, bn), BF16)) \
        + 4 * _nbytes((bm, bn), F32)
    return pl.pallas_call(
        _gate_up_kernel,
        out_shape=jax.ShapeDtypeStruct((m, n), BF16),
        grid=(m // bm, n // bn),
        in_specs=[pl.BlockSpec((bm, kdim), lambda i, j: (i, 0)), w_spec, w_spec],
        out_specs=pl.BlockSpec((bm, bn), lambda i, j: (i, j)),
        compiler_params=_params(("parallel", "parallel"), vmem),
        name="ffn_gate_up",
    )(a, wg, wu)


def _ln_kernel(z_ref, g_ref, b_ref, of_ref, *maybe_ob_ref):
    z = z_ref[...]
    mu = jnp.mean(z, axis=-1, keepdims=True)
    zc = z - mu
    var = jnp.mean(zc * zc, axis=-1, keepdims=True)
    out = zc * lax.rsqrt(var + LN_EPS) * g_ref[...] + b_ref[...]
    of_ref[...] = out
    for ob_ref in maybe_ob_ref:
        ob_ref[...] = out.astype(BF16)


def _layernorm(z, g, b, br, with_bf16=True):
    s, d = z.shape
    row = pl.BlockSpec((br, d), lambda i: (i, 0))
    vec = pl.BlockSpec((1, d), lambda i: (0, 0))
    vmem = 2 * (2 * _nbytes((br, d), F32) + _nbytes((br, d), BF16)) + 4 * _nbytes((br, d), F32)
    out_shape = [jax.ShapeDtypeStruct((s, d), F32)] + [jax.ShapeDtypeStruct((s, d), BF16)] * with_bf16
    outs = pl.pallas_call(
        _ln_kernel,
        out_shape=tuple(out_shape),
        grid=(s // br,),
        in_specs=[row, vec, vec],
        out_specs=tuple([row] * len(out_shape)),
        compiler_params=_params(("parallel",), vmem),
        name="layernorm",
    )(z, g.reshape(1, d), b.reshape(1, d))
    return outs if with_bf16 else (outs[0], None)


def _local_kernel(u_ref, uh_ref, hc_ref, hch_ref, bg_ref, cg_ref, cgh_ref, wp_ref, ps_ref, cw_ref,
                  ya_ref, yc_ref, us_ref, zs_ref, *, bs, group):
    i = pl.program_id(0)
    has_hist = i > 0
    us_ref[0:HALO, :] = jnp.where(has_hist, uh_ref[...].astype(F32), 0.0)
    us_ref[HALO:, :] = u_ref[...].astype(F32)
    zs_ref[0:HALO, :] = jnp.where(has_hist, cgh_ref[...].astype(F32) * hch_ref[...].astype(F32), 0.0)
    zs_ref[HALO:, :] = cg_ref[...].astype(F32) * hc_ref[...].astype(F32)

    pos = i * bs + lax.broadcasted_iota(jnp.int32, (bs, group), 0)
    for g, w in enumerate(POOL_WINDOWS):
        cols = slice(g * group, (g + 1) * group)
        cur = us_ref[HALO:HALO + bs, cols]
        acc = cur
        for j in range(1, w):
            acc = acc + us_ref[HALO - j:HALO - j + bs, cols]
        count = jnp.minimum(pos + 1, w).astype(F32)
        diff = acc / count - cur
        y = jnp.dot(diff.astype(BF16), wp_ref[g], preferred_element_type=F32)
        ya_ref[:, cols] = (y * ps_ref[:, cols]).astype(ya_ref.dtype)

    conv = cw_ref[0:1, :] * zs_ref[HALO - 2:HALO - 2 + bs, :]
    for tap in range(1, CONV_TAPS):
        off = HALO - (CONV_TAPS - 1) + tap
        conv = conv + cw_ref[tap:tap + 1, :] * zs_ref[off:off + bs, :]
    yc_ref[...] = (bg_ref[...].astype(F32) * conv).astype(yc_ref.dtype)


def _local_mixers(proj, w_pool, pool_scale, conv_w, width, bs):
    s = proj.shape[0]
    group = width // len(POOL_WINDOWS)
    hb = bs // HALO

    def blk(c):
        return pl.BlockSpec((bs, width), lambda i, c=c: (i, c))

    def halo(c):
        return pl.BlockSpec((HALO, width), lambda i, c=c: (jnp.maximum(i * hb - 1, 0), c))

    out = pl.BlockSpec((bs, width), lambda i: (i, 0))
    vmem = 2 * (6 * _nbytes((bs, width), BF16) + _nbytes(w_pool.shape, BF16)) \
        + 2 * _nbytes((bs + HALO, width), F32) + 6 * _nbytes((bs, width), F32)
    return pl.pallas_call(
        functools.partial(_local_kernel, bs=bs, group=group),
        out_shape=(jax.ShapeDtypeStruct((s, width), BF16), jax.ShapeDtypeStruct((s, width), BF16)),
        grid=(s // bs,),
        in_specs=[blk(0), halo(0), blk(4), halo(4), blk(5), blk(6), halo(6),
                  pl.BlockSpec(w_pool.shape, lambda i: (0, 0, 0)),
                  pl.BlockSpec((1, width), lambda i: (0, 0)),
                  pl.BlockSpec((CONV_TAPS, width), lambda i: (0, 0))],
        out_specs=(out, out),
        scratch_shapes=[pltpu.VMEM((bs + HALO, width), F32), pltpu.VMEM((bs + HALO, width), F32)],
        compiler_params=_params(("arbitrary",), vmem),
        name="pool_conv_mixers",
    )(proj, proj, proj, proj, proj, proj, proj, w_pool, pool_scale.reshape(1, width), conv_w)


def _t5_bucket(dist):
    exact = N_BUCKETS // 2
    d = jnp.maximum(dist, 1).astype(F32)
    large = exact + (jnp.log(d / exact) / math.log(T5_MAX_DIST / exact) * (N_BUCKETS - exact)).astype(jnp.int32)
    return jnp.where(dist < exact, dist, jnp.minimum(large, N_BUCKETS - 1))


def _band_bias(t5_bias, window, dilation):
    steps = window // dilation
    qi = jnp.arange(BLOCK)[:, None]
    ki = jnp.arange(2 * BLOCK)[None, :]
    dist = qi + BLOCK - ki
    bucket = _t5_bucket(jnp.clip(dist, 0) * dilation)
    hit = bucket[..., None, None] == jnp.arange(N_BUCKETS)[:, None]
    bias = jnp.sum(jnp.where(hit, t5_bias.astype(F32)[None, None], 0.0), axis=2)
    bias = jnp.where(((dist >= 0) & (dist <= steps))[..., None], bias, NEG)
    return jnp.moveaxis(bias, -1, 0)


def _dilated_kernel(q_ref, kc_ref, kp_ref, vc_ref, vp_ref, bias_ref, o_ref,
                    q_scr, k_scr, v_scr, part_o, part_l, *, tile, branches):
    t = pl.program_id(1)
    q_scr[...] = q_ref[...].astype(F32) * (HEAD_DIM ** -0.5 * LOG2_E)
    k_scr[0:tile, :] = kp_ref[...].astype(F32)
    k_scr[tile:, :] = kc_ref[...].astype(F32)
    v_scr[0:tile, :] = vp_ref[...].astype(F32)
    v_scr[tile:, :] = vc_ref[...].astype(F32)
    before_start = (t == 0) & (lax.broadcasted_iota(jnp.int32, (BLOCK, 2 * BLOCK), 1) < BLOCK)

    def rows_of(start, size, d):
        return pl.ds(start, size) if d == 1 else pl.ds(start, size, stride=d)

    for idx, d in enumerate(branches):
        last = idx == len(branches) - 1
        for r in range(d):
            for b in range(tile // (d * BLOCK)):
                q_rows = rows_of(r + d * BLOCK * b, BLOCK, d)
                k_rows = rows_of(tile + r + d * BLOCK * (b - 1), 2 * BLOCK, d)
                s = lax.dot_general(q_scr[q_rows, :].astype(BF16), k_scr[k_rows, :].astype(BF16),
                                    (((1,), (1,)), ((), ())), preferred_element_type=F32)
                s = s + bias_ref[idx]
                if b == 0:
                    s = jnp.where(before_start, NEG, s)
                m = jnp.max(s, axis=-1, keepdims=True)
                p = jnp.exp2(s - m)
                den = jnp.sum(p, axis=-1, keepdims=True)
                out = jnp.dot(p.astype(BF16), v_scr[k_rows, :].astype(BF16), preferred_element_type=F32) / den
                lse = m + jnp.log2(den)
                if not last:
                    part_o[idx, q_rows, :] = out
                    part_l[idx, q_rows, :] = jnp.broadcast_to(lse, (BLOCK, HEAD_DIM))
                else:
                    lses = [part_l[e, q_rows, :] for e in range(idx)] + [lse]
                    outs = [part_o[e, q_rows, :] for e in range(idx)] + [out]
                    top = functools.reduce(jnp.maximum, lses)
                    wts = [jnp.exp2(x - top) for x in lses]
                    num = functools.reduce(lambda x, y: x + y, [w * o for w, o in zip(wts, outs)])
                    o_ref[q_rows, :] = (num / functools.reduce(lambda x, y: x + y, wts)).astype(o_ref.dtype)


def _dilated_attention(proj, t5_bias, width):
    s = proj.shape[0]
    heads = width // HEAD_DIM
    order = sorted(DIL_PATTERNS, key=lambda p: -p[1])
    branches = tuple(d for _, d in order)
    tile = max(branches) * BLOCK
    assert branches[-1] == 1 and s % tile == 0 and all(w // d == BLOCK and tile % (d * BLOCK) == 0 for w, d in order)
    bias = jnp.stack([_band_bias(t5_bias, w, d) for w, d in order], axis=1) * LOG2_E

    def cur(c):
        return pl.BlockSpec((tile, HEAD_DIM), lambda h, t, c=c: (t, c * heads + h))

    def prev(c):
        return pl.BlockSpec((tile, HEAD_DIM), lambda h, t, c=c: (jnp.maximum(t - 1, 0), c * heads + h))

    nb = len(branches)
    vmem = 2 * (6 * _nbytes((tile, HEAD_DIM), BF16) + _nbytes(bias.shape[1:], F32)) \
        + (5 + 2 * (nb - 1)) * _nbytes((tile, HEAD_DIM), F32) + 64 * _nbytes((BLOCK, 2 * BLOCK), F32)
    return pl.pallas_call(
        functools.partial(_dilated_kernel, tile=tile, branches=branches),
        out_shape=jax.ShapeDtypeStruct((s, width), BF16),
        grid=(heads, s // tile),
        in_specs=[cur(1), cur(2), prev(2), cur(3), prev(3),
                  pl.BlockSpec((None,) + bias.shape[1:], lambda h, t: (h, 0, 0, 0))],
        out_specs=pl.BlockSpec((tile, HEAD_DIM), lambda h, t: (t, h)),
        scratch_shapes=[pltpu.VMEM((tile, HEAD_DIM), F32), pltpu.VMEM((2 * tile, HEAD_DIM), F32),
                        pltpu.VMEM((2 * tile, HEAD_DIM), F32),
                        pltpu.VMEM((nb - 1, tile, HEAD_DIM), F32), pltpu.VMEM((nb - 1, tile, HEAD_DIM), F32)],
        compiler_params=_params(("parallel", "arbitrary"), vmem),
        name="dilated_attention",
    )(proj, proj, proj, proj, proj, bias)


def _rms(x, g):
    return x * lax.rsqrt(jnp.mean(x * x, axis=-1, keepdims=True) + RMS_EPS) * g


def _rope_chunk(chunk, cos_ref, sin_ref):
    half = ROPE_DIM // 2
    swapped = pltpu.roll(chunk, HEAD_DIM - half, 1) + pltpu.roll(chunk, half, 1)
    return chunk * cos_ref[...] + swapped * sin_ref[...]


def _mla_q_kernel(cq_ref, g_ref, w_ref, cos_ref, sin_ref, qt_ref, *, heads, scale):
    qn = _rms(cq_ref[...], g_ref[...]).astype(BF16)
    q = jnp.dot(qn, w_ref[...], preferred_element_type=F32)
    for h in range(heads):
        nope = q[:, h * MLA_QK_PAD:h * MLA_QK_PAD + HEAD_DIM]
        rope = _rope_chunk(q[:, h * MLA_QK_PAD + HEAD_DIM:(h + 1) * MLA_QK_PAD], cos_ref, sin_ref)
        qt_ref[h * MLA_QK_PAD:h * MLA_QK_PAD + HEAD_DIM, :] = (nope * scale).T.astype(qt_ref.dtype)
        qt_ref[h * MLA_QK_PAD + HEAD_DIM:(h + 1) * MLA_QK_PAD, :] = (rope * scale).T.astype(qt_ref.dtype)


def _mla_kv_kernel(ckv_ref, kr_ref, g_ref, w_ref, cos_ref, sin_ref, k_ref, vt_ref, *, heads, bk):
    kvn = _rms(ckv_ref[...], g_ref[...]).astype(BF16)
    kv = jnp.dot(kvn, w_ref[...], preferred_element_type=F32)
    k_rope = _rope_chunk(kr_ref[...], cos_ref, sin_ref).astype(k_ref.dtype)
    for h in range(heads):
        k_ref[:, h * MLA_QK_PAD:h * MLA_QK_PAD + HEAD_DIM] = \
            kv[:, h * HEAD_DIM:(h + 1) * HEAD_DIM].astype(k_ref.dtype)
        k_ref[:, h * MLA_QK_PAD + HEAD_DIM:(h + 1) * MLA_QK_PAD] = k_rope
        for c in range(vt_ref.shape[1]):
            v = kv[c * bk:(c + 1) * bk, (heads + h) * HEAD_DIM:(heads + h + 1) * HEAD_DIM]
            vt_ref[h, c, 0:HEAD_DIM, :] = v.T.astype(vt_ref.dtype)
            vt_ref[h, c, HEAD_DIM:, :] = jnp.ones((ONES_ROWS, bk), vt_ref.dtype)


def _mla_flash_kernel(qt_ref, k_ref, vt_ref, o_ref, m_scr, acc_scr, s_scr, smax_scr, *, bq, bk):
    chains = bk // bq
    i = pl.program_id(1)
    m_scr[...] = jnp.full(m_scr.shape, NEG, F32)
    acc_scr[...] = jnp.zeros(acc_scr.shape, F32)
    qts = [qt_ref[:, c * bq:(c + 1) * bq] for c in range(chains)]

    def keys(j):
        return k_ref[pl.ds(pl.multiple_of(j * bk, bk), bk), :]

    def logits(c, k_blk):
        s = jnp.dot(k_blk, qts[c], preferred_element_type=F32)
        return s, jnp.max(s, axis=0, keepdims=True)

    def consume(c, s, s_max, vt_blk, diagonal):
        if diagonal:
            key = lax.broadcasted_iota(jnp.int32, s.shape, 0)
            qry = lax.broadcasted_iota(jnp.int32, s.shape, 1) + c * bq
            s = jnp.where(key <= qry, s, NEG)
            s_max = jnp.max(s, axis=0, keepdims=True)
        m_prev = m_scr[c]
        m_new = jnp.maximum(m_prev, s_max)
        alpha = jnp.exp2(m_prev - m_new)
        p = jnp.exp2(s - m_new).astype(BF16)
        acc_scr[c] = alpha * acc_scr[c] + jnp.dot(vt_blk, p, preferred_element_type=F32)
        m_scr[c] = m_new

    k_first = keys(0)
    for c in range(chains):
        s_scr[c], smax_scr[c] = logits(c, k_first)

    def body(j, carry):
        k_next = keys(j + 1)
        vt_blk = vt_ref[j]
        for c in range(chains):
            s, s_max = s_scr[c], smax_scr[c]
            s_scr[c], smax_scr[c] = logits(c, k_next)
            consume(c, s, s_max, vt_blk, False)
        return carry

    lax.fori_loop(0, i, body, 0)
    vt_blk = vt_ref[i]
    for c in range(chains):
        live = (c + 1) * bq
        consume(c, s_scr[c, 0:live, :], None, vt_blk[:, 0:live], True)
        acc = acc_scr[c]
        o_ref[c * bq:(c + 1) * bq, :] = (acc[0:HEAD_DIM] / acc[HEAD_DIM:HEAD_DIM + 1]).T.astype(o_ref.dtype)


def _rope_tables(s):
    half = ROPE_DIM // 2
    pos = jnp.arange(s, dtype=F32)
    inv = ROPE_THETA ** (-jnp.arange(0, ROPE_DIM, 2, dtype=F32) / ROPE_DIM)
    ang = pos[:, None] * inv[None, :]
    cos, sin = jnp.cos(ang), jnp.sin(ang)
    zeros = jnp.zeros((s, HEAD_DIM - 2 * half), F32)
    return jnp.concatenate([cos, cos, zeros], axis=1), jnp.concatenate([-sin, sin, zeros], axis=1)


def _mla(lat, q_rank, kv_rank, q_norm_g, w_uq_p, kv_norm_g, w_ukv_p, cos_t, sin_t, heads):
    s = lat.shape[0]
    bm = _pick(s, (1024, 512, 256, 128))
    bq = _pick(s, (512, 256, 128))
    bk = 2 * bq
    assert (q_rank - kv_rank) % HEAD_DIM == 0 and kv_rank % HEAD_DIM == 0 and bm % bk == 0 and s % bk == 0
    tab = pl.BlockSpec((bm, HEAD_DIM), lambda i: (i, 0))
    qk_width = heads * MLA_QK_PAD
    v_width = heads * HEAD_DIM
    vt_rows = HEAD_DIM + ONES_ROWS
    scale = (HEAD_DIM + ROPE_DIM) ** -0.5 * LOG2_E

    vmem_q = 2 * (_nbytes((bm, q_rank), F32) + _nbytes(w_uq_p.shape, BF16) + 2 * _nbytes((bm, HEAD_DIM), F32)
                  + _nbytes((bm, qk_width), BF16)) + 3 * _nbytes((bm, qk_width), F32)
    qt = pl.pallas_call(
        functools.partial(_mla_q_kernel, heads=heads, scale=scale),
        out_shape=jax.ShapeDtypeStruct((qk_width, s), BF16),
        grid=(s // bm,),
        in_specs=[pl.BlockSpec((bm, q_rank), lambda i: (i, 1)),
                  pl.BlockSpec((1, q_rank), lambda i: (0, 0)),
                  pl.BlockSpec(w_uq_p.shape, lambda i: (0, 0)), tab, tab],
        out_specs=pl.BlockSpec((qk_width, bm), lambda i: (0, i)),
        compiler_params=_params(("parallel",), vmem_q),
        name="mla_q_up",
    )(lat, q_norm_g.reshape(1, q_rank), w_uq_p, cos_t, sin_t)

    vmem_kv = 2 * (_nbytes((bm, kv_rank), F32) + _nbytes(w_ukv_p.shape, BF16) + 3 * _nbytes((bm, HEAD_DIM), F32)
                   + _nbytes((bm, qk_width), BF16) + _nbytes((bm, v_width), BF16)) \
        + 3 * _nbytes((bm, 2 * v_width), F32)
    k, vt = pl.pallas_call(
        functools.partial(_mla_kv_kernel, heads=heads, bk=bk),
        out_shape=(jax.ShapeDtypeStruct((s, qk_width), BF16),
                   jax.ShapeDtypeStruct((heads, s // bk, vt_rows, bk), BF16)),
        grid=(s // bm,),
        in_specs=[pl.BlockSpec((bm, kv_rank), lambda i: (i, 0)),
                  pl.BlockSpec((bm, HEAD_DIM), lambda i: (i, kv_rank // HEAD_DIM)),
                  pl.BlockSpec((1, kv_rank), lambda i: (0, 0)),
                  pl.BlockSpec(w_ukv_p.shape, lambda i: (0, 0)), tab, tab],
        out_specs=(pl.BlockSpec((bm, qk_width), lambda i: (i, 0)),
                   pl.BlockSpec((heads, bm // bk, vt_rows, bk), lambda i: (0, i, 0, 0))),
        compiler_params=_params(("parallel",), vmem_kv),
        name="mla_kv_up",
    )(lat, lat, kv_norm_g.reshape(1, kv_rank), w_ukv_p, cos_t, sin_t)

    chains = bk // bq
    vmem_a = 2 * (_nbytes((MLA_QK_PAD, bk), BF16) + _nbytes((s, MLA_QK_PAD), BF16) + _nbytes((s, vt_rows), BF16)
                  + _nbytes((bk, HEAD_DIM), BF16)) + chains * (_nbytes((vt_rows, bq), F32) + 5 * _nbytes((bk, bq), F32))
    return pl.pallas_call(
        functools.partial(_mla_flash_kernel, bq=bq, bk=bk),
        out_shape=jax.ShapeDtypeStruct((s, v_width), BF16),
        grid=(heads, s // bk),
        in_specs=[pl.BlockSpec((MLA_QK_PAD, bk), lambda h, i: (h, i)),
                  pl.BlockSpec((s, MLA_QK_PAD), lambda h, i: (0, h)),
                  pl.BlockSpec((None, s // bk, vt_rows, bk), lambda h, i: (h, 0, 0, 0))],
        out_specs=pl.BlockSpec((bk, HEAD_DIM), lambda h, i: (i, h)),
        scratch_shapes=[pltpu.VMEM((chains, 1, bq), F32), pltpu.VMEM((chains, vt_rows, bq), F32),
                        pltpu.VMEM((chains, bk, bq), F32), pltpu.VMEM((chains, 1, bq), F32)],
        compiler_params=_params(("parallel", "arbitrary"), vmem_a),
        name="mla_flash_attention",
    )(qt, k, vt)


def _prep_weights(d, w_in, w_uq, w_ukv, heads):
    depth = w_in.shape[0]
    q_rank = w_uq.shape[1]
    kv_rank = w_ukv.shape[1]
    n_main = 7 * (d // 4)
    w_main = w_in[:, :, :n_main].astype(BF16)
    cq = w_in[:, :, n_main:n_main + q_rank]
    ckv = w_in[:, :, n_main + q_rank:n_main + q_rank + kv_rank]
    kr = w_in[:, :, n_main + q_rank + kv_rank:]
    kr = jnp.pad(kr, ((0, 0), (0, 0), (0, q_rank - kv_rank - kr.shape[2])))
    w_lat = jnp.concatenate([ckv, kr, cq], axis=2).astype(BF16)
    uq = w_uq.reshape(depth, q_rank, heads, HEAD_DIM + ROPE_DIM)
    uq = jnp.pad(uq, ((0, 0), (0, 0), (0, 0), (0, MLA_QK_PAD - HEAD_DIM - ROPE_DIM)))
    uq = uq.reshape(depth, q_rank, heads * MLA_QK_PAD).astype(BF16)
    ukv = w_ukv.reshape(depth, kv_rank, heads, 2, HEAD_DIM).transpose(0, 1, 3, 2, 4)
    ukv = ukv.reshape(depth, kv_rank, 2 * heads * HEAD_DIM).astype(BF16)
    return w_main, w_lat, uq, ukv


def kernel(x, w_in, w_pool, pool_scale, t5_bias, conv_w, q_norm_g, w_uq, kv_norm_g, w_ukv, w_out, ln1_g, ln1_b,
           w_gate, w_up, w_down, ln2_g, ln2_b):
    batch, s, d = x.shape
    depth = w_in.shape[0]
    width = d // 4
    heads = (d - 3 * width) // HEAD_DIM
    q_rank = w_uq.shape[1]
    kv_rank = w_ukv.shape[1]
    f = w_gate.shape[-1]
    alpha = (2 * depth) ** 0.25
    assert batch == 1 and width % (len(POOL_WINDOWS) * HEAD_DIM) == 0 and f % 256 == 0

    bm = _pick(s, (1024, 512, 256, 128))
    cos_t, sin_t = _rope_tables(s)
    w_main_b, w_lat_b, w_uq_b, w_ukv_b = _prep_weights(d, w_in, w_uq, w_ukv, heads)
    w_out_b, w_gate_b, w_up_b, w_down_b = (w.astype(BF16) for w in (w_out, w_gate, w_up, w_down))
    xf = x.reshape(s, d)
    xb = xf.astype(BF16)
    for l in range(depth):
        proj = _matmul(xb, w_main_b, l, BF16, bm, _pick(7 * width, (1024, 896, 512, 256, 128)), name="in_proj_main")
        lat = _matmul(xb, w_lat_b, l, F32, bm, _pick(2 * q_rank, (768, 512, 384, 256, 128)), name="in_proj_latent")
        y_a, y_c = _local_mixers(proj, w_pool[l].astype(BF16), pool_scale[l], conv_w[l], width,
                                 _pick(s, (512, 256, 128)))
        y_b = _dilated_attention(proj, t5_bias, width)
        y_d = _mla(lat, q_rank, kv_rank, q_norm_g[l], w_uq_b[l], kv_norm_g[l], w_ukv_b[l], cos_t, sin_t, heads)
        z = _matmul([y_a, y_b, y_c, y_d], w_out_b, l, F32, bm, _pick(d, (1024, 512, 256, 128)), residual=xf, alpha=alpha,
                    name="out_proj")
        xf, xb = _layernorm(z, ln1_g[l], ln1_b[l], _pick(s, (256, 128)))
        hid = _gate_up(xb, w_gate_b, w_up_b, l, _pick(s, (2048, 1024, 512, 256, 128)), 256)
        z = _matmul(hid, w_down_b, l, F32, _pick(s, (512, 256, 128)), _pick(d, (512, 256, 128)), residual=xf,
                    alpha=alpha, name="ffn_down")
        xf, xb = _layernorm(z, ln2_g[l], ln2_b[l], _pick(s, (256, 128)), with_bf16=l + 1 < depth)
    return xf.reshape(batch, s, d)
```
